```python
import math
import jax, jax.numpy as jnp
from jax import lax
import numpy as np

D_MODEL = 2048
BATCH = 2
SEQ = 8192
DEPTH = 4

HEAD_DIM = 128
N_HEADS = D_MODEL // HEAD_DIM
N_HEADS_A = N_HEADS // 4
N_HEADS_B = N_HEADS - N_HEADS_A
WIDTH_A = N_HEADS_A * HEAD_DIM
WIDTH_B = N_HEADS_B * HEAD_DIM
PROJ_WIDTH = 3 * WIDTH_B + 2 * WIDTH_A
CHUNK = 128
DILATED = ((128, 1), (512, 4), (2048, 16))
RADIUS = 64
ROT_DIM = HEAD_DIM // 4
ROPE_THETA = 500000.0
D_FF_DENSE = 5632
N_EXPERTS = 8
TOP_K = 2
D_FF_EXPERT = 7168
N_DENSE = (DEPTH + 1) // 2
N_MOE = DEPTH // 2
EPS = 1e-5
NEG = -1e30

kernel_name = "hybrid_gmlp_dilated_attn_moe_encoder"


def rms_norm(x, g):
    xf = x.astype(jnp.float32)
    y = xf * lax.rsqrt(jnp.mean(xf * xf, axis=-1, keepdims=True) + EPS)
    return (y * g.astype(jnp.float32)).astype(x.dtype)


def rope_tables(positions):
    inv = ROPE_THETA ** (-jnp.arange(0, ROT_DIM, 2, dtype=jnp.float32) / ROT_DIM)
    ang = positions.astype(jnp.float32)[..., None] * inv
    return jnp.cos(ang)[:, :, None, :], jnp.sin(ang)[:, :, None, :]


def partial_rope(t, cos, sin):
    half = ROT_DIM // 2
    tr = t[..., :ROT_DIM].astype(jnp.float32)
    t1, t2 = tr[..., :half], tr[..., half:]
    rot = jnp.concatenate([t1 * cos - t2 * sin, t2 * cos + t1 * sin], axis=-1)
    return jnp.concatenate([rot.astype(t.dtype), t[..., ROT_DIM:]], axis=-1)


def dilated_window_branch(q, k, v, dil):
    B, S, H, E = q.shape
    L = S // dil
    blk = RADIUS
    nb = -(-L // blk)
    Lp = nb * blk

    def split(t):
        return t.reshape(B, L, dil, H, E).transpose(0, 2, 3, 1, 4)

    qs = jnp.pad(split(q), ((0, 0), (0, 0), (0, 0), (0, Lp - L), (0, 0)))
    kv_pad = ((0, 0), (0, 0), (0, 0), (blk, Lp - L + blk), (0, 0))
    ks = jnp.pad(split(k), kv_pad)
    vs = jnp.pad(split(v), kv_pad)

    def band(t):
        return jnp.concatenate(
            [t[..., o:o + Lp, :].reshape(B, dil, H, nb, blk, E) for o in (0, blk, 2 * blk)], axis=4)

    qb = qs.reshape(B, dil, H, nb, blk, E).astype(jnp.float32)
    kb = band(ks).astype(jnp.float32)
    vb = band(vs).astype(jnp.float32)
    s = jnp.einsum('bdhnqe,bdhnke->bdhnqk', qb, kb) * (E ** -0.5)
    mq = jnp.arange(nb)[:, None] * blk + jnp.arange(blk)[None, :]
    mk = jnp.arange(nb)[:, None] * blk - blk + jnp.arange(3 * blk)[None, :]
    rel = mk[:, None, :] - mq[:, :, None]
    valid = (jnp.abs(rel) <= RADIUS) & (mk[:, None, :] >= 0) & (mk[:, None, :] < L)
    s = jnp.where(valid, s, NEG)
    m = jnp.max(s, axis=-1, keepdims=True)
    p = jnp.exp(s - m)
    den = jnp.sum(p, axis=-1)
    o = jnp.einsum('bdhnqk,bdhnke->bdhnqe', p, vb) / den[..., None]
    lse = m[..., 0] + jnp.log(den)
    o = o.reshape(B, dil, H, Lp, E)[:, :, :, :L].transpose(0, 3, 1, 2, 4).reshape(B, S, H, E)
    lse = lse.reshape(B, dil, H, Lp)[..., :L].transpose(0, 3, 1, 2).reshape(B, S, H)
    return o, lse


def dilated_attention(q, k, v):
    outs, lses = [], []
    for (_, dil) in DILATED:
        o, l = dilated_window_branch(q, k, v, dil)
        outs.append(o)
        lses.append(l)
    w = jax.nn.softmax(jnp.stack(lses, axis=0), axis=0)
    o = jnp.sum(w[..., None] * jnp.stack(outs, axis=0), axis=0)
    return o.astype(q.dtype)


def spatial_gating(u, v, w_s, b_s, g_v):
    B, S, H, E = u.shape
    u = jax.nn.gelu(u)
    v = rms_norm(jax.nn.gelu(v), g_v)
    vc = v.reshape(B, S // CHUNK, CHUNK, H, E)
    sv = jnp.einsum('hpq,bcqhe->bcphe', w_s, vc) + jnp.transpose(b_s)[None, None, :, :, None]
    return u * sv.reshape(B, S, H, E)


def swiglu(t, w1, w3, w2):
    return (jax.nn.silu(t @ w1) * (t @ w3)) @ w2


def moe_swiglu(h, router, w1, w3, w2):
    shp = h.shape
    t = h.reshape(-1, shp[-1])
    logits = (t @ router).astype(jnp.float32)
    vals, idx = lax.top_k(logits, TOP_K)
    gates = jax.nn.softmax(vals, axis=-1)
    cw = jnp.sum(jax.nn.one_hot(idx, N_EXPERTS, dtype=jnp.float32) * gates[..., None], axis=1)
    y = jnp.zeros(t.shape, jnp.float32)
    for e in range(N_EXPERTS):
        y = y + cw[:, e:e + 1] * swiglu(t, w1[e], w3[e], w2[e]).astype(jnp.float32)
    return y.astype(h.dtype).reshape(shp)


def mixer(h, cos, sin, w_in, w_s, b_s, g_v, g_out, w_out):
    B, S, _ = h.shape
    proj = h @ w_in
    o = 0
    q = proj[..., o:o + WIDTH_B].reshape(B, S, N_HEADS_B, HEAD_DIM); o += WIDTH_B
    k = proj[..., o:o + WIDTH_B].reshape(B, S, N_HEADS_B, HEAD_DIM); o += WIDTH_B
    v = proj[..., o:o + WIDTH_B].reshape(B, S, N_HEADS_B, HEAD_DIM); o += WIDTH_B
    ua = proj[..., o:o + WIDTH_A].reshape(B, S, N_HEADS_A, HEAD_DIM); o += WIDTH_A
    va = proj[..., o:o + WIDTH_A].reshape(B, S, N_HEADS_A, HEAD_DIM)
    q = partial_rope(q, cos, sin)
    k = partial_rope(k, cos, sin)
    y_b = dilated_attention(q, k, v)
    y_a = spatial_gating(ua, va, w_s, b_s, g_v)
    y = jnp.concatenate([y_a, y_b], axis=2)
    y = rms_norm(y, g_out).reshape(B, S, N_HEADS * HEAD_DIM)
    return y @ w_out


def setup_inputs(seed: int = 0) -> dict:
    key = jax.random.key(seed)
    ks = jax.random.split(key, 20)
    f32 = jnp.float32
    nrm = lambda k, shp, fan: jax.random.normal(k, shp, f32) * (fan ** -0.5)
    gain = lambda k, shp: 1.0 + 0.02 * jax.random.normal(k, shp, f32)
    x = jax.random.normal(ks[0], (BATCH, SEQ, D_MODEL), f32)
    positions = (jnp.arange(SEQ, dtype=jnp.int32)[None, :]
                 + jax.random.randint(ks[1], (BATCH, 1), 0, 4096, dtype=jnp.int32))
    return {
        "x": x,
        "positions": positions,
        "attn_norm": gain(ks[2], (DEPTH, D_MODEL)),
        "w_in": nrm(ks[3], (DEPTH, D_MODEL, PROJ_WIDTH), D_MODEL),
        "w_s": nrm(ks[4], (DEPTH, N_HEADS_A, CHUNK, CHUNK), CHUNK),
        "b_s": gain(ks[5], (DEPTH, N_HEADS_A, CHUNK)),
        "gmlp_norm": gain(ks[6], (DEPTH, N_HEADS_A, HEAD_DIM)),
        "out_norm": gain(ks[7], (DEPTH, N_HEADS, HEAD_DIM)),
        "w_out": nrm(ks[8], (DEPTH, D_MODEL, D_MODEL), D_MODEL),
        "ffn_norm": gain(ks[9], (DEPTH, D_MODEL)),
        "dense_w1": nrm(ks[10], (N_DENSE, D_MODEL, D_FF_DENSE), D_MODEL),
        "dense_w3": nrm(ks[11], (N_DENSE, D_MODEL, D_FF_DENSE), D_MODEL),
        "dense_w2": nrm(ks[12], (N_DENSE, D_FF_DENSE, D_MODEL), D_FF_DENSE),
        "router": nrm(ks[13], (N_MOE, D_MODEL, N_EXPERTS), D_MODEL),
        "moe_w1": nrm(ks[14], (N_MOE, N_EXPERTS, D_MODEL, D_FF_EXPERT), D_MODEL),
        "moe_w3": nrm(ks[15], (N_MOE, N_EXPERTS, D_MODEL, D_FF_EXPERT), D_MODEL),
        "moe_w2": nrm(ks[16], (N_MOE, N_EXPERTS, D_FF_EXPERT, D_MODEL), D_FF_EXPERT),
        "final_norm": gain(ks[17], (D_MODEL,)),
    }


def reference(x, positions, attn_norm, w_in, w_s, b_s, gmlp_norm, out_norm, w_out,
              ffn_norm, dense_w1, dense_w3, dense_w2, router, moe_w1, moe_w3, moe_w2,
              final_norm):
    cos, sin = rope_tables(positions)
    for i in range(DEPTH):
        h = rms_norm(x, attn_norm[i])
        x = x + mixer(h, cos, sin, w_in[i], w_s[i], b_s[i], gmlp_norm[i], out_norm[i], w_out[i])
        h = rms_norm(x, ffn_norm[i])
        j = i // 2
        if i % 2 == 0:
            x = x + swiglu(h, dense_w1[j], dense_w3[j], dense_w2[j])
        else:
            x = x + moe_swiglu(h, router[j], moe_w1[j], moe_w3[j], moe_w2[j])
    return rms_norm(x, final_norm)
```

```python
import functools
import math

import numpy as np
import jax
import jax.numpy as jnp
from jax import lax
from jax.experimental import pallas as pl
from jax.experimental.pallas import tpu as pltpu

HEAD_DIM = 128
CHUNK = 128
RADIUS = 64
DILATIONS = (16, 4, 1)
ROT_DIM = HEAD_DIM // 4
ROPE_THETA = 500000.0
TOP_K = 2
EPS = 1e-5
NEG = -1e30
LANES = 128
VMEM_LIMIT_BYTES = 56 * 1024 * 1024

F32 = jnp.float32
BF16 = jnp.bfloat16


def _pick(n, prefs):
    for p in prefs:
        if n % p == 0:
            return p
    raise ValueError(f"no tile in {prefs} divides {n}")


def _params(*sem):
    return pltpu.CompilerParams(dimension_semantics=sem, vmem_limit_bytes=VMEM_LIMIT_BYTES)


def _rms(x, g):
    ms = jnp.mean(x * x, axis=-1, keepdims=True)
    return x * lax.rsqrt(ms + EPS) * g


def _gelu(x):
    return 0.5 * x * (1.0 + jnp.tanh(math.sqrt(2.0 / math.pi) * (x + 0.044715 * (x * x * x))))


def _silu(x):
    return x / (1.0 + jnp.exp(-x))


def _rope_kernel(pos_ref, inv_ref, c_ref, s1_ref, s2_ref):
    ang = pos_ref[...].astype(F32) * inv_ref[...]
    lane = lax.broadcasted_iota(jnp.int32, ang.shape, 1)
    sn = jnp.sin(ang)
    c_ref[...] = jnp.cos(ang)
    s1_ref[...] = jnp.where(lane < ROT_DIM // 2, -sn, 0.0)
    s2_ref[...] = jnp.where(lane >= ROT_DIM // 2, sn, 0.0)


def _rope_tables(positions):
    n = positions.size
    half = ROT_DIM // 2
    inv = ROPE_THETA ** (-np.arange(0, ROT_DIM, 2, dtype=np.float32) / ROT_DIM)
    inv_lane = np.zeros((1, LANES), np.float32)
    inv_lane[0, :half] = inv
    inv_lane[0, half:ROT_DIM] = inv
    tm = _pick(n, (2048, 1024, 512, 256, 128))
    out = jax.ShapeDtypeStruct((n, LANES), F32)
    return pl.pallas_call(
        _rope_kernel,
        grid=(n // tm,),
        in_specs=[pl.BlockSpec((tm, 1), lambda i: (i, 0)),
                  pl.BlockSpec((1, LANES), lambda i: (0, 0))],
        out_specs=[pl.BlockSpec((tm, LANES), lambda i: (i, 0))] * 3,
        out_shape=[out, out, out],
        compiler_params=_params("parallel"),
        name="rope_tables",
    )(positions.reshape(n, 1), jnp.asarray(inv_lane))


def _norm_kernel(x_ref, g_ref, o_ref):
    o_ref[...] = _rms(x_ref[...], g_ref[...]).astype(o_ref.dtype)


def _norm(x, g):
    n, d = x.shape
    tm = _pick(n, (512, 256, 128))
    return pl.pallas_call(
        _norm_kernel,
        grid=(n // tm,),
        in_specs=[pl.BlockSpec((tm, d), lambda i: (i, 0)),
                  pl.BlockSpec((1, d), lambda i: (0, 0))],
        out_specs=pl.BlockSpec((tm, d), lambda i: (i, 0)),
        out_shape=jax.ShapeDtypeStruct((n, d), BF16),
        compiler_params=_params("parallel"),
        name="first_norm",
    )(x, g.reshape(1, d))


def _proj_kernel(a_ref, w_ref, c_ref, s1_ref, s2_ref, o_ref, *, n_q_tiles, n_rope_tiles, heads_per_tile):
    j = pl.program_id(1)
    acc = jnp.dot(a_ref[...], w_ref[...].astype(BF16), preferred_element_type=F32)

    @pl.when(j < n_rope_tiles)
    def _():
        c = c_ref[...]
        s1 = s1_ref[...]
        s2 = s2_ref[...]
        scale = jnp.where(j < n_q_tiles, HEAD_DIM ** -0.5, 1.0).astype(F32)
        for h in range(heads_per_tile):
            t = acc[:, h * LANES:(h + 1) * LANES]
            lo = pltpu.roll(t, LANES - ROT_DIM // 2, 1)
            hi = pltpu.roll(t, ROT_DIM // 2, 1)
            o_ref[h] = (t * c + lo * s1 + hi * s2) * scale

    @pl.when(j >= n_rope_tiles)
    def _():
        for h in range(heads_per_tile):
            o_ref[h] = acc[:, h * LANES:(h + 1) * LANES]


def _project(h, w_in, layer, tabs, width_b):
    n, d = h.shape
    p = w_in.shape[2]
    tm = _pick(n, (1024, 512, 256, 128))
    tn = _pick(math.gcd(width_b, p), (512, 256, 128))
    hpt = tn // LANES
    kern = functools.partial(_proj_kernel, n_q_tiles=width_b // tn, n_rope_tiles=2 * width_b // tn,
                             heads_per_tile=hpt)
    tab_spec = pl.BlockSpec((tm, LANES), lambda i, j: (i, 0))
    return pl.pallas_call(
        kern,
        grid=(n // tm, p // tn),
        in_specs=[pl.BlockSpec((tm, d), lambda i, j: (i, 0)),
                  pl.BlockSpec((None, d, tn), lambda i, j: (layer, 0, j)),
                  tab_spec, tab_spec, tab_spec],
        out_specs=pl.BlockSpec((hpt, tm, LANES), lambda i, j: (j, i, 0)),
        out_shape=jax.ShapeDtypeStruct((p // LANES, n, LANES), F32),
        compiler_params=_params("parallel", "arbitrary"),
        name="in_proj",
    )(h, w_in, *tabs)


def _attn_kernel(q_ref, k_ref, v_ref, g_ref, o_ref, acc_ref, m_ref, l_ref, *, seq, tq):
    win = tq + 2 * RADIUS
    reps = win // LANES
    qq = lax.broadcasted_iota(jnp.int32, (tq, win), 0)
    kk = lax.broadcasted_iota(jnp.int32, (tq, win), 1)
    dkq = kk - qq

    for d in DILATIONS:
        sub_len = seq // d
        nt = sub_len // tq
        first = d == DILATIONS[0]
        last = d == DILATIONS[-1]

        def body(t, carry, d=d, sub_len=sub_len, nt=nt, first=first, last=last):
            r = t // nt
            q0 = (t % nt) * tq
            k0 = jnp.clip(q0 - RADIUS, 0, sub_len - win)
            if d == 1:
                qs = pl.ds(pl.multiple_of(q0, tq), tq)
                ks = pl.ds(pl.multiple_of(k0, RADIUS), win)
            else:
                qs = pl.ds(r + d * q0, tq, stride=d)
                ks = pl.ds(r + d * k0, win, stride=d)
            q = q_ref[qs, :].astype(BF16)
            k = k_ref[ks, :].astype(BF16)
            v = v_ref[ks, :].astype(BF16)
            s = lax.dot_general(q, k, (((1,), (1,)), ((), ())), preferred_element_type=F32)
            s = jnp.where(jnp.abs(dkq + (k0 - q0)) <= RADIUS, s, NEG)
            m_t = jnp.max(s, axis=-1, keepdims=True)
            if first:
                m_new = jnp.broadcast_to(m_t, (tq, LANES))
            else:
                m_prev = m_ref[qs, :]
                m_new = jnp.maximum(m_prev, m_t)
            p = jnp.exp(s - jnp.concatenate([m_new] * reps, axis=1))
            l_t = jnp.sum(p, axis=-1, keepdims=True)
            pv = jnp.dot(p.astype(BF16), v, preferred_element_type=F32)
            if first:
                l_new = jnp.broadcast_to(l_t, (tq, LANES))
                acc_new = pv
            else:
                alpha = jnp.exp(m_prev - m_new)
                l_new = alpha * l_ref[qs, :] + l_t
                acc_new = alpha * acc_ref[qs, :] + pv
            if last:
                o_ref[qs, :] = _rms(acc_new / l_new, g_ref[...]).astype(o_ref.dtype)
            else:
                m_ref[qs, :] = m_new
                l_ref[qs, :] = l_new
                acc_ref[qs, :] = acc_new
            return carry

        lax.fori_loop(0, seq // tq, body, 0)


def _attention(proj, g_out, batch, seq, n_heads_a, n_heads_b):
    tq = 128
    assert seq % (max(DILATIONS) * tq) == 0 and seq // max(DILATIONS) >= tq + 2 * RADIUS
    n = batch * seq
    kern = functools.partial(_attn_kernel, seq=seq, tq=tq)
    blk = lambda off: pl.BlockSpec((None, seq, LANES), lambda b, h: (off + h, b, 0))
    return pl.pallas_call(
        kern,
        grid=(batch, n_heads_b),
        in_specs=[blk(0), blk(n_heads_b), blk(2 * n_heads_b),
                  pl.BlockSpec((None, 1, LANES), lambda b, h: (n_heads_a + h, 0, 0))],
        out_specs=pl.BlockSpec((seq, LANES), lambda b, h: (b, h)),
        out_shape=jax.ShapeDtypeStruct((n, n_heads_b * LANES), BF16),
        scratch_shapes=[pltpu.VMEM((seq, LANES), F32)] * 3,
        compiler_params=_params("parallel", "parallel"),
        name="dilated_attention",
    )(proj, proj, proj, g_out.reshape(-1, 1, LANES))


def _gmlp_kernel(u_ref, v_ref, ws_ref, bs_ref, gv_ref, go_ref, o_ref, *, n_heads_a, n_chunks):
    for h in range(n_heads_a):
        w = ws_ref[h].astype(BF16)
        b = bs_ref[h]
        gv = gv_ref[h]
        go = go_ref[h]
        for c in range(n_chunks):
            rows = slice(c * CHUNK, (c + 1) * CHUNK)
            u = _gelu(u_ref[h, rows, :])
            v = _rms(_gelu(v_ref[h, rows, :]), gv)
            sv = jnp.dot(w, v.astype(BF16), preferred_element_type=F32) + b
            o_ref[rows, h * LANES:(h + 1) * LANES] = _rms(u * sv, go).astype(o_ref.dtype)


def _gmlp(proj, w_s, b_s, g_v, g_out, n_heads_a, n_heads_b):
    n = proj.shape[1]
    tt = _pick(n, (512, 256, 128))
    assert (3 * n_heads_b) % n_heads_a == 0
    u_blk = 3 * n_heads_b // n_heads_a
    kern = functools.partial(_gmlp_kernel, n_heads_a=n_heads_a, n_chunks=tt // CHUNK)
    full = lambda shp: pl.BlockSpec(shp, lambda i: (0,) * len(shp))
    return pl.pallas_call(
        kern,
        grid=(n // tt,),
        in_specs=[pl.BlockSpec((n_heads_a, tt, LANES), lambda i: (u_blk, i, 0)),
                  pl.BlockSpec((n_heads_a, tt, LANES), lambda i: (u_blk + 1, i, 0)),
                  full((n_heads_a, CHUNK, CHUNK)),
                  full((n_heads_a, CHUNK, 1)),
                  full((n_heads_a, 1, LANES)),
                  full((n_heads_a, 1, LANES))],
        out_specs=pl.BlockSpec((tt, n_heads_a * LANES), lambda i: (i, 0)),
        out_shape=jax.ShapeDtypeStruct((n, n_heads_a * LANES), BF16),
        compiler_params=_params("parallel"),
        name="gmlp_gating",
    )(proj, proj, w_s, b_s.reshape(n_heads_a, CHUNK, 1), g_v.reshape(n_heads_a, 1, LANES),
      g_out[:n_heads_a].reshape(n_heads_a, 1, LANES))


def _route(hn, r_ref, n_experts):
    logits = jnp.dot(hn, r_ref[...], precision=lax.Precision.HIGHEST, preferred_element_type=F32)
    lane = lax.broadcasted_iota(jnp.int32, logits.shape, 1)
    lg = jnp.where(lane < n_experts, logits, -jnp.inf)
    v1 = jnp.max(lg, axis=-1, keepdims=True)
    i1 = jnp.min(jnp.where(lg == v1, lane, LANES), axis=-1, keepdims=True)
    lg2 = jnp.where(lane == i1, -jnp.inf, lg)
    v2 = jnp.max(lg2, axis=-1, keepdims=True)
    i2 = jnp.min(jnp.where(lg2 == v2, lane, LANES), axis=-1, keepdims=True)
    e2 = jnp.exp(v2 - v1)
    g1 = 1.0 / (1.0 + e2)
    g2 = e2 / (1.0 + e2)
    return jnp.where(lane == 0, i1.astype(F32),
                     jnp.where(lane == 1, i2.astype(F32),
                               jnp.where(lane == 2, g1, jnp.where(lane == 3, g2, 0.0))))


def _finish(xn, g_ref, r_ref, xo_ref, h_ref, route_ref, n_experts):
    xo_ref[...] = xn
    hn = _rms(xn, g_ref[...])
    h_ref[...] = hn.astype(h_ref.dtype)
    if route_ref is not None:
        route_ref[...] = _route(hn, r_ref, n_experts)


def _out_kernel(*refs, n_experts):
    if n_experts:
        ya_ref, yb_ref, w_ref, x_ref, g_ref, r_ref, xo_ref, h_ref, route_ref, wb_ref = refs
    else:
        ya_ref, yb_ref, w_ref, x_ref, g_ref, xo_ref, h_ref, wb_ref = refs
        r_ref = route_ref = None

    @pl.when(pl.program_id(0) == 0)
    def _():
        wb_ref[...] = w_ref[...].astype(BF16)

    wa = ya_ref.shape[1]
    acc = jnp.dot(ya_ref[...], wb_ref[:wa, :], preferred_element_type=F32)
    acc += jnp.dot(yb_ref[...], wb_ref[wa:, :], preferred_element_type=F32)
    _finish(x_ref[...] + acc, g_ref, r_ref, xo_ref, h_ref, route_ref, n_experts)


def _pad_router(router):
    d, e = router.shape
    return jnp.zeros((d, LANES), F32).at[:, :e].set(router)


def _out_proj(ya, yb, w_out, layer, x, g, router, h_dtype):
    n, d = x.shape
    tm = _pick(n, (256, 128))
    row = lambda w: pl.BlockSpec((tm, w), lambda i: (i, 0))
    in_specs = [row(ya.shape[1]), row(yb.shape[1]),
                pl.BlockSpec((None, d, d), lambda i: (layer, 0, 0), pipeline_mode=pl.Buffered(1)),
                row(d), pl.BlockSpec((1, d), lambda i: (0, 0))]
    args = [ya, yb, w_out, x, g.reshape(1, d)]
    out_specs = [row(d), row(d)]
    out_shape = [jax.ShapeDtypeStruct((n, d), F32), jax.ShapeDtypeStruct((n, d), h_dtype)]
    n_experts = 0
    if router is not None:
        n_experts = router.shape[1]
        in_specs.append(pl.BlockSpec((d, LANES), lambda i: (0, 0)))
        args.append(_pad_router(router))
        out_specs.append(row(LANES))
        out_shape.append(jax.ShapeDtypeStruct((n, LANES), F32))
    return pl.pallas_call(
        functools.partial(_out_kernel, n_experts=n_experts),
        grid=(n // tm,),
        in_specs=in_specs,
        out_specs=out_specs,
        out_shape=out_shape,
        scratch_shapes=[pltpu.VMEM((d, d), BF16)],
        compiler_params=_params("arbitrary"),
        name="out_proj",
    )(*args)


def _up_kernel(a_ref, w1_ref, w3_ref, o_ref):
    a = a_ref[...]
    h1 = jnp.dot(a, w1_ref[...].astype(BF16), preferred_element_type=F32)
    h3 = jnp.dot(a, w3_ref[...].astype(BF16), preferred_element_type=F32)
    o_ref[...] = (_silu(h1) * h3).astype(o_ref.dtype)


def _up_dense(h, w1, w3, layer):
    n, d = h.shape
    f = w1.shape[2]
    tm = _pick(n, (1024, 512, 256, 128))
    tn = _pick(f, (512, 256, 128))
    w_spec = pl.BlockSpec((None, d, tn), lambda i, j: (layer, 0, j))
    return pl.pallas_call(
        _up_kernel,
        grid=(n // tm, f // tn),
        in_specs=[pl.BlockSpec((tm, d), lambda i, j: (i, 0)), w_spec, w_spec],
        out_specs=pl.BlockSpec((tm, tn), lambda i, j: (i, j)),
        out_shape=jax.ShapeDtypeStruct((n, f), BF16),
        compiler_params=_params("parallel", "arbitrary"),
        name="dense_up",
    )(h, w1, w3)


def _up_grouped_kernel(te_ref, act_ref, a_ref, w1_ref, w3_ref, o_ref, ab_ref):
    i = pl.program_id(0)

    @pl.when(act_ref[i] == 1)
    def _():
        @pl.when(pl.program_id(1) == 0)
        def _():
            ab_ref[...] = a_ref[...].astype(BF16)

        a = ab_ref[...]
        h1 = jnp.dot(a, w1_ref[...].astype(BF16), preferred_element_type=F32)
        h3 = jnp.dot(a, w3_ref[...].astype(BF16), preferred_element_type=F32)
        o_ref[...] = (_silu(h1) * h3).astype(o_ref.dtype)

    @pl.when(act_ref[i] == 0)
    def _():
        o_ref[...] = jnp.zeros_like(o_ref)


def _up_grouped(xs, w1, w3, layer, tile_expert, tile_active, tm):
    rp, d = xs.shape
    f = w1.shape[3]
    tn = _pick(f, (512, 256, 128))
    w_spec = pl.BlockSpec((None, None, d, tn), lambda i, j, te, act: (layer, te[i], 0, j))
    grid_spec = pltpu.PrefetchScalarGridSpec(
        num_scalar_prefetch=2,
        grid=(rp // tm, f // tn),
        in_specs=[pl.BlockSpec((tm, d), lambda i, j, te, act: (i, 0)), w_spec, w_spec],
        out_specs=pl.BlockSpec((tm, tn), lambda i, j, te, act: (i, j)),
        scratch_shapes=[pltpu.VMEM((tm, d), BF16)],
    )
    return pl.pallas_call(
        _up_grouped_kernel,
        grid_spec=grid_spec,
        out_shape=jax.ShapeDtypeStruct((rp, f), BF16),
        compiler_params=_params("parallel", "arbitrary"),
        name="moe_up",
    )(tile_expert, tile_active, xs, w1, w3)


def _down_kernel(a_ref, w_ref, x_ref, g_ref, xo_ref, h_ref, acc_ref, *, nk):
    k = pl.program_id(1)
    part = jnp.dot(a_ref[...], w_ref[...].astype(BF16), preferred_element_type=F32)

    @pl.when(k == 0)
    def _():
        acc_ref[...] = part

    @pl.when(k > 0)
    def _():
        acc_ref[...] += part

    @pl.when(k == nk - 1)
    def _():
        _finish(x_ref[...] + acc_ref[...], g_ref, None, xo_ref, h_ref, None, 0)


def _down_dense(a, w2, layer, x, g, h_dtype):
    n, d = x.shape
    f = a.shape[1]
    tm = _pick(n, (512, 256, 128))
    tk = _pick(f, (512, 256, 128))
    row = pl.BlockSpec((tm, d), lambda i, k: (i, 0))
    return pl.pallas_call(
        functools.partial(_down_kernel, nk=f // tk),
        grid=(n // tm, f // tk),
        in_specs=[pl.BlockSpec((tm, tk), lambda i, k: (i, k)),
                  pl.BlockSpec((None, tk, d), lambda i, k: (layer, k, 0)),
                  row, pl.BlockSpec((1, d), lambda i, k: (0, 0))],
        out_specs=[row, row],
        out_shape=[jax.ShapeDtypeStruct((n, d), F32), jax.ShapeDtypeStruct((n, d), h_dtype)],
        scratch_shapes=[pltpu.VMEM((tm, d), F32)],
        compiler_params=_params("parallel", "arbitrary"),
        name="dense_down",
    )(a, w2, x, g.reshape(1, d))


def _down_grouped_kernel(te_ref, act_ref, a_ref, w_ref, o_ref, *, nk):
    i = pl.program_id(0)
    k = pl.program_id(1)

    @pl.when(act_ref[i] == 1)
    def _():
        part = jnp.dot(a_ref[...], w_ref[...].astype(BF16), preferred_element_type=F32)

        @pl.when(k == 0)
        def _():
            o_ref[...] = part

        @pl.when(k > 0)
        def _():
            o_ref[...] += part

    @pl.when(jnp.logical_and(act_ref[i] == 0, k == 0))
    def _():
        o_ref[...] = jnp.zeros_like(o_ref)


def _down_grouped(a, w2, layer, tile_expert, tile_active, tm):
    rp, f = a.shape
    d = w2.shape[3]
    tk = _pick(f, (512, 256, 128))
    grid_spec = pltpu.PrefetchScalarGridSpec(
        num_scalar_prefetch=2,
        grid=(rp // tm, f // tk),
        in_specs=[pl.BlockSpec((tm, tk), lambda i, k, te, act: (i, k)),
                  pl.BlockSpec((None, None, tk, d), lambda i, k, te, act: (layer, te[i], k, 0))],
        out_specs=pl.BlockSpec((tm, d), lambda i, k, te, act: (i, 0)),
    )
    return pl.pallas_call(
        functools.partial(_down_grouped_kernel, nk=f // tk),
        grid_spec=grid_spec,
        out_shape=jax.ShapeDtypeStruct((rp, d), F32),
        compiler_params=_params("parallel", "arbitrary"),
        name="moe_down",
    )(tile_expert, tile_active, a, w2)


def _row_copy(src_hbm, row, dst_ref, dst_row, sem):
    return pltpu.make_async_copy(src_hbm.at[pl.ds(row, 1)], dst_ref.at[pl.ds(dst_row, 1)], sem)


def _gather_kernel(src_ref, h_hbm, o_ref, sem, *, tg):
    base = pl.program_id(0) * tg

    def start(r, c):
        _row_copy(h_hbm, src_ref[base + r], o_ref, r, sem).start()
        return c

    def wait(r, c):
        _row_copy(h_hbm, 0, o_ref, r, sem).wait()
        return c

    lax.fori_loop(0, tg, start, 0)
    lax.fori_loop(0, tg, wait, 0)


def _gather_rows(h, src):
    rp = src.shape[0]
    d = h.shape[1]
    tg = _pick(rp, (256, 128))
    grid_spec = pltpu.PrefetchScalarGridSpec(
        num_scalar_prefetch=1,
        grid=(rp // tg,),
        in_specs=[pl.BlockSpec(memory_space=pl.ANY)],
        out_specs=pl.BlockSpec((tg, d), lambda i, src: (i, 0)),
        scratch_shapes=[pltpu.SemaphoreType.DMA(())],
    )
    return pl.pallas_call(
        functools.partial(_gather_kernel, tg=tg),
        grid_spec=grid_spec,
        out_shape=jax.ShapeDtypeStruct((rp, d), h.dtype),
        compiler_params=_params("arbitrary"),
        name="moe_gather",
    )(src, h)


def _combine_kernel(pos_ref, x_ref, route_ref, ys_hbm, g_ref, xo_ref, h_ref, buf_ref, sem, *, tt):
    base = pl.program_id(0) * tt

    def start(r, c):
        for k in range(TOP_K):
            _row_copy(ys_hbm, pos_ref[TOP_K * (base + r) + k], buf_ref.at[k], r, sem.at[k]).start()
        return c

    def wait(r, c):
        for k in range(TOP_K):
            _row_copy(ys_hbm, 0, buf_ref.at[k], r, sem.at[k]).wait()
        return c

    lax.fori_loop(0, tt, start, 0)
    lax.fori_loop(0, tt, wait, 0)
    route = route_ref[...]
    xn = x_ref[...] + route[:, 2:3] * buf_ref[0] + route[:, 3:4] * buf_ref[1]
    _finish(xn, g_ref, None, xo_ref, h_ref, None, 0)


def _combine(x, route, ys, pos, g, h_dtype):
    n, d = x.shape
    tt = _pick(n, (256, 128))
    row = pl.BlockSpec((tt, d), lambda i, pos: (i, 0))
    grid_spec = pltpu.PrefetchScalarGridSpec(
        num_scalar_prefetch=1,
        grid=(n // tt,),
        in_specs=[row, pl.BlockSpec((tt, LANES), lambda i, pos: (i, 0)),
                  pl.BlockSpec(memory_space=pl.ANY),
                  pl.BlockSpec((1, d), lambda i, pos: (0, 0))],
        out_specs=[row, row],
        scratch_shapes=[pltpu.VMEM((TOP_K, tt, d), F32), pltpu.SemaphoreType.DMA((TOP_K,))],
    )
    return pl.pallas_call(
        functools.partial(_combine_kernel, tt=tt),
        grid_spec=grid_spec,
        out_shape=[jax.ShapeDtypeStruct((n, d), F32), jax.ShapeDtypeStruct((n, d), h_dtype)],
        compiler_params=_params("arbitrary"),
        name="moe_combine",
    )(pos, x, route, ys, g.reshape(1, d))


def _dispatch_plan(route, n_experts, tm):
    n = route.shape[0]
    ef = route[:, :TOP_K].astype(jnp.int32).reshape(-1)
    onehot = (ef[:, None] == jnp.arange(n_experts, dtype=jnp.int32)[None, :]).astype(jnp.int32)
    csum = jnp.cumsum(onehot, axis=0)
    counts = csum[-1]
    rank = jnp.take_along_axis(csum, ef[:, None], axis=1)[:, 0] - 1
    padded = ((counts + tm - 1) // tm) * tm
    gend = jnp.cumsum(padded)
    gstart = gend - padded
    pos = (gstart[ef] + rank).astype(jnp.int32)
    rp = TOP_K * n + n_experts * tm
    src = jnp.zeros((rp,), jnp.int32).at[pos].set(jnp.arange(TOP_K * n, dtype=jnp.int32) // TOP_K)
    tile_start = jnp.arange(rp // tm, dtype=jnp.int32) * tm
    tile_expert = jnp.minimum(jnp.searchsorted(gend, tile_start, side="right"), n_experts - 1).astype(jnp.int32)
    tile_active = (tile_start < gend[-1]).astype(jnp.int32)
    return pos, src, tile_expert, tile_active


def kernel(x, positions, attn_norm, w_in, w_s, b_s, gmlp_norm, out_norm, w_out, ffn_norm, dense_w1, dense_w3,
           dense_w2, router, moe_w1, moe_w3, moe_w2, final_norm):
    batch, seq, d = x.shape
    depth = w_in.shape[0]
    n = batch * seq
    n_heads = d // HEAD_DIM
    n_heads_a = n_heads // 4
    n_heads_b = n_heads - n_heads_a
    width_b = n_heads_b * HEAD_DIM

    xf = x.reshape(n, d)
    tabs = _rope_tables(positions)
    h = _norm(xf, attn_norm[0])
    for i in range(depth):
        moe = i % 2 == 1
        j = i // 2
        last = i == depth - 1
        next_g = final_norm if last else attn_norm[i + 1]
        next_dtype = F32 if last else BF16

        proj = _project(h, w_in, i, tabs, width_b)
        yb = _attention(proj, out_norm[i], batch, seq, n_heads_a, n_heads_b)
        ya = _gmlp(proj, w_s[i], b_s[i], gmlp_norm[i], out_norm[i], n_heads_a, n_heads_b)
        if not moe:
            xf, h2 = _out_proj(ya, yb, w_out, i, xf, ffn_norm[i], None, BF16)
            mid = _up_dense(h2, dense_w1, dense_w3, j)
            xf, h = _down_dense(mid, dense_w2, j, xf, next_g, next_dtype)
        else:
            n_experts = router.shape[2]
            tm = _pick(n, (1024, 512, 256, 128))
            xf, h2, route = _out_proj(ya, yb, w_out, i, xf, ffn_norm[i], router[j], F32)
            pos, src, tile_expert, tile_active = _dispatch_plan(route, n_experts, tm)
            xs = _gather_rows(h2, src)
            mid = _up_grouped(xs, moe_w1, moe_w3, j, tile_expert, tile_active, tm)
            ys = _down_grouped(mid, moe_w2, j, tile_expert, tile_active, tm)
            xf, h = _combine(xf, route, ys, pos, next_g, next_dtype)
    return h.reshape(batch, seq, d)
```

```python
import functools
import math

import numpy as np
import jax
import jax.numpy as jnp
from jax import lax
from jax.experimental import pallas as pl
from jax.experimental.pallas import tpu as pltpu

HEAD_DIM = 128
CHUNK = 128
RADIUS = 64
DILATIONS = (16, 4, 1)
ROT_DIM = HEAD_DIM // 4
ROPE_THETA = 500000.0
TOP_K = 2
EPS = 1e-5
NEG = -1e30
LANES = 128
VMEM_LIMIT_BYTES = 56 * 1024 * 1024
ROW_CHUNK = 256

F32 = jnp.float32
BF16 = jnp.bfloat16
U32 = jnp.uint32


def _pick(n, prefs):
    for p in prefs:
        if n % p == 0:
            return p
    raise ValueError(f"no tile in {prefs} divides {n}")


def _params(*sem):
    return pltpu.CompilerParams(dimension_semantics=sem, vmem_limit_bytes=VMEM_LIMIT_BYTES)


def _rms(x, g):
    ms = jnp.mean(x * x, axis=-1, keepdims=True)
    return x * lax.rsqrt(ms + EPS) * g


def _gelu(x):
    return 0.5 * x * (1.0 + jnp.tanh(math.sqrt(2.0 / math.pi) * (x + 0.044715 * (x * x * x))))


def _silu(x):
    return x / (1.0 + jnp.exp(-x))


def _chunks(tm):
    cr = ROW_CHUNK if tm % ROW_CHUNK == 0 else tm
    return [slice(c * cr, (c + 1) * cr) for c in range(tm // cr)]


def _rope_kernel(pos_ref, inv_ref, c_ref, s1_ref, s2_ref):
    ang = pos_ref[...].astype(F32) * inv_ref[...]
    lane = lax.broadcasted_iota(jnp.int32, ang.shape, 1)
    sn = jnp.sin(ang)
    c_ref[...] = jnp.cos(ang)
    s1_ref[...] = jnp.where(lane < ROT_DIM // 2, -sn, 0.0)
    s2_ref[...] = jnp.where(lane >= ROT_DIM // 2, sn, 0.0)


def _rope_tables(positions):
    n = positions.size
    half = ROT_DIM // 2
    inv = ROPE_THETA ** (-np.arange(0, ROT_DIM, 2, dtype=np.float32) / ROT_DIM)
    inv_lane = np.zeros((1, LANES), np.float32)
    inv_lane[0, :half] = inv
    inv_lane[0, half:ROT_DIM] = inv
    tm = _pick(n, (2048, 1024, 512, 256, 128))
    out = jax.ShapeDtypeStruct((n, LANES), F32)
    return pl.pallas_call(
        _rope_kernel,
        grid=(n // tm,),
        in_specs=[pl.BlockSpec((tm, 1), lambda i: (i, 0)),
                  pl.BlockSpec((1, LANES), lambda i: (0, 0))],
        out_specs=[pl.BlockSpec((tm, LANES), lambda i: (i, 0))] * 3,
        out_shape=[out, out, out],
        compiler_params=_params("parallel"),
        name="rope_tables",
    )(positions.reshape(n, 1), jnp.asarray(inv_lane))


def _norm_kernel(x_ref, g_ref, o_ref):
    o_ref[...] = _rms(x_ref[...], g_ref[...]).astype(o_ref.dtype)


def _norm(x, g, dtype):
    n, d = x.shape
    tm = _pick(n, (512, 256, 128))
    return pl.pallas_call(
        _norm_kernel,
        grid=(n // tm,),
        in_specs=[pl.BlockSpec((tm, d), lambda i: (i, 0)),
                  pl.BlockSpec((1, d), lambda i: (0, 0))],
        out_specs=pl.BlockSpec((tm, d), lambda i: (i, 0)),
        out_shape=jax.ShapeDtypeStruct((n, d), dtype),
        compiler_params=_params("parallel"),
        name="row_norm",
    )(x, g.reshape(1, d))


def _proj_kernel(a_ref, w_ref, c_ref, s1_ref, s2_ref, o_ref, wb_ref, *, n_q_tiles, n_rope_tiles, heads_per_tile,
                 q_scale):
    j = pl.program_id(0)

    @pl.when(pl.program_id(1) == 0)
    def _():
        wb_ref[...] = w_ref[...].astype(BF16)

    def emit(rope):
        for rows in _chunks(a_ref.shape[0]):
            acc = jnp.dot(a_ref[rows, :], wb_ref[...], preferred_element_type=F32)
            if not rope:
                for h in range(heads_per_tile):
                    o_ref[h, rows, :] = acc[:, h * LANES:(h + 1) * LANES]
                continue
            c = c_ref[rows, :]
            s1 = s1_ref[rows, :]
            s2 = s2_ref[rows, :]
            scale = jnp.where(j < n_q_tiles, q_scale, 1.0).astype(F32)
            for h in range(heads_per_tile):
                t = acc[:, h * LANES:(h + 1) * LANES]
                lo = pltpu.roll(t, LANES - ROT_DIM // 2, 1)
                hi = pltpu.roll(t, ROT_DIM // 2, 1)
                o_ref[h, rows, :] = (t * c + lo * s1 + hi * s2) * scale

    pl.when(j < n_rope_tiles)(lambda: emit(True))
    pl.when(j >= n_rope_tiles)(lambda: emit(False))


def _project(h, w_in, layer, tabs, width_b):
    n, d = h.shape
    p = w_in.shape[2]
    tm = _pick(n, (512, 256, 128))
    tn = _pick(math.gcd(width_b, p), (512, 256, 128))
    hpt = tn // LANES
    kern = functools.partial(_proj_kernel, n_q_tiles=width_b // tn, n_rope_tiles=2 * width_b // tn,
                             heads_per_tile=hpt, q_scale=math.log2(math.e) * HEAD_DIM ** -0.5)
    tab_spec = pl.BlockSpec((tm, LANES), lambda j, i: (i, 0))
    return pl.pallas_call(
        kern,
        grid=(p // tn, n // tm),
        in_specs=[pl.BlockSpec((tm, d), lambda j, i: (i, 0)),
                  pl.BlockSpec((None, d, tn), lambda j, i: (layer, 0, j)),
                  tab_spec, tab_spec, tab_spec],
        out_specs=pl.BlockSpec((hpt, tm, LANES), lambda j, i: (j, i, 0)),
        out_shape=jax.ShapeDtypeStruct((p // LANES, n, LANES), F32),
        scratch_shapes=[pltpu.VMEM((d, tn), BF16)],
        compiler_params=_params("arbitrary", "arbitrary"),
        name="in_proj",
    )(h, w_in, *tabs)


def _attn_kernel(q_ref, k_ref, v_ref, g_ref, o_ref, qd_ref, kd_ref, vd_ref, bias_ref, acc_ref, m_ref, l_ref, *,
                 seq, tq, unroll):
    win = tq + 2 * RADIUS
    copy_rows = 256
    qq = lax.broadcasted_iota(jnp.int32, (tq, win), 0)
    kk = lax.broadcasted_iota(jnp.int32, (tq, win), 1)
    for case in range(3):
        bias_ref[case] = jnp.where(jnp.abs(kk - qq - case * RADIUS) <= RADIUS, 0.0, NEG)
    ones = jnp.ones((win, LANES), BF16)

    for d in DILATIONS:
        sub_len = seq // d
        nt = sub_len // tq
        first = d == DILATIONS[0]
        last = d == DILATIONS[-1]

        def deinterleave(c, carry, d=d, sub_len=sub_len):
            per = sub_len // copy_rows
            r = c // per
            m0 = (c % per) * copy_rows
            src = pl.ds(r + d * m0, copy_rows, stride=d) if d > 1 else pl.ds(pl.multiple_of(m0, copy_rows), copy_rows)
            dst = pl.ds(pl.multiple_of(c * copy_rows, copy_rows), copy_rows)
            qd_ref[dst, :] = q_ref[src, :].astype(BF16)
            kd_ref[dst, :] = k_ref[src, :].astype(BF16)
            vd_ref[dst, :] = v_ref[src, :].astype(BF16)
            return carry

        lax.fori_loop(0, seq // copy_rows, deinterleave, 0)

        def tile(t, d=d, sub_len=sub_len, nt=nt):
            r = t // nt
            ti = t % nt
            q0 = ti * tq
            k0 = jnp.clip(q0 - RADIUS, 0, sub_len - win)
            case = jnp.where(ti == 0, 0, jnp.where(ti == nt - 1, 2, 1))
            kbase = pl.multiple_of(r * sub_len + k0, RADIUS)
            q = qd_ref[pl.ds(pl.multiple_of(t * tq, tq), tq), :]
            k = kd_ref[pl.ds(kbase, win), :]
            v = vd_ref[pl.ds(kbase, win), :]
            s = lax.dot_general(q, k, (((1,), (1,)), ((), ())), preferred_element_type=F32) + bias_ref[case]
            m_t = jnp.max(s, axis=-1, keepdims=True)
            p = jnp.exp2(s - m_t).astype(BF16)
            pv = jnp.dot(p, jnp.concatenate([v, ones], axis=1), preferred_element_type=F32)
            if d == 1:
                rows = pl.ds(pl.multiple_of(q0, tq), tq)
            else:
                rows = pl.ds(r + d * q0, tq, stride=d)
            return rows, jnp.broadcast_to(m_t, (tq, LANES)), pv[:, LANES:], pv[:, :LANES]

        def merge(rows, m_t, l_t, acc_t, first=first, last=last):
            if first:
                m_new, l_new, acc_new = m_t, l_t, acc_t
            else:
                m_prev = m_ref[rows, :]
                m_new = jnp.maximum(m_prev, m_t)
                a = jnp.exp2(m_prev - m_new)
                b = jnp.exp2(m_t - m_new)
                l_new = a * l_ref[rows, :] + b * l_t
                acc_new = a * acc_ref[rows, :] + b * acc_t
            if last:
                o_ref[rows, :] = _rms(acc_new / l_new, g_ref[...]).astype(o_ref.dtype)
            else:
                m_ref[rows, :] = m_new
                l_ref[rows, :] = l_new
                acc_ref[rows, :] = acc_new

        def body(it, carry):
            parts = [tile(it * unroll + u) for u in range(unroll)]
            for part in parts:
                merge(*part)
            return carry

        lax.fori_loop(0, seq // (tq * unroll), body, 0)


def _attention(proj, g_out, batch, seq, n_heads_a, n_heads_b):
    tq = 128
    unroll = 8
    win = tq + 2 * RADIUS
    assert seq % (max(DILATIONS) * tq) == 0 and (seq // tq) % unroll == 0 and seq // max(DILATIONS) >= max(win, 256)
    n = batch * seq
    kern = functools.partial(_attn_kernel, seq=seq, tq=tq, unroll=unroll)
    blk = lambda off: pl.BlockSpec((None, seq, LANES), lambda b, h: (off + h, b, 0))
    return pl.pallas_call(
        kern,
        grid=(batch, n_heads_b),
        in_specs=[blk(0), blk(n_heads_b), blk(2 * n_heads_b),
                  pl.BlockSpec((None, 1, LANES), lambda b, h: (n_heads_a + h, 0, 0))],
        out_specs=pl.BlockSpec((seq, LANES), lambda b, h: (b, h)),
        out_shape=jax.ShapeDtypeStruct((n, n_heads_b * LANES), BF16),
        scratch_shapes=[pltpu.VMEM((seq, LANES), BF16)] * 3
        + [pltpu.VMEM((3, tq, win), F32)]
        + [pltpu.VMEM((seq, LANES), F32)] * 3,
        compiler_params=_params("parallel", "parallel"),
        name="dilated_attention",
    )(proj, proj, proj, g_out.reshape(-1, 1, LANES))


def _gmlp_kernel(u_ref, v_ref, ws_ref, bs_ref, gv_ref, go_ref, o_ref, *, n_heads_a, n_chunks):
    for h in range(n_heads_a):
        w = ws_ref[h].astype(BF16)
        b = bs_ref[h]
        gv = gv_ref[h]
        go = go_ref[h]
        for c in range(n_chunks):
            rows = slice(c * CHUNK, (c + 1) * CHUNK)
            u = _gelu(u_ref[h, rows, :])
            v = _rms(_gelu(v_ref[h, rows, :]), gv)
            sv = jnp.dot(w, v.astype(BF16), preferred_element_type=F32) + b
            o_ref[rows, h * LANES:(h + 1) * LANES] = _rms(u * sv, go).astype(o_ref.dtype)


def _gmlp(proj, w_s, b_s, g_v, g_out, n_heads_a, n_heads_b):
    n = proj.shape[1]
    tt = _pick(n, (512, 256, 128))
    assert (3 * n_heads_b) % n_heads_a == 0
    u_blk = 3 * n_heads_b // n_heads_a
    kern = functools.partial(_gmlp_kernel, n_heads_a=n_heads_a, n_chunks=tt // CHUNK)
    full = lambda shp: pl.BlockSpec(shp, lambda i: (0,) * len(shp))
    return pl.pallas_call(
        kern,
        grid=(n // tt,),
        in_specs=[pl.BlockSpec((n_heads_a, tt, LANES), lambda i: (u_blk, i, 0)),
                  pl.BlockSpec((n_heads_a, tt, LANES), lambda i: (u_blk + 1, i, 0)),
                  full((n_heads_a, CHUNK, CHUNK)),
                  full((n_heads_a, CHUNK, 1)),
                  full((n_heads_a, 1, LANES)),
                  full((n_heads_a, 1, LANES))],
        out_specs=pl.BlockSpec((tt, n_heads_a * LANES), lambda i: (i, 0)),
        out_shape=jax.ShapeDtypeStruct((n, n_heads_a * LANES), BF16),
        compiler_params=_params("parallel"),
        name="gmlp_gating",
    )(proj, proj, w_s, b_s.reshape(n_heads_a, CHUNK, 1), g_v.reshape(n_heads_a, 1, LANES),
      g_out[:n_heads_a].reshape(n_heads_a, 1, LANES))


def _route(hn, r_ref, n_experts):
    logits = jnp.dot(hn, r_ref[...], precision=lax.Precision.HIGHEST, preferred_element_type=F32)
    lane = lax.broadcasted_iota(jnp.int32, logits.shape, 1)
    lg = jnp.where(lane < n_experts, logits, -jnp.inf)
    v1 = jnp.max(lg, axis=-1, keepdims=True)
    i1 = jnp.min(jnp.where(lg == v1, lane, LANES), axis=-1, keepdims=True)
    lg2 = jnp.where(lane == i1, -jnp.inf, lg)
    v2 = jnp.max(lg2, axis=-1, keepdims=True)
    i2 = jnp.min(jnp.where(lg2 == v2, lane, LANES), axis=-1, keepdims=True)
    e2 = jnp.exp(v2 - v1)
    g1 = 1.0 / (1.0 + e2)
    g2 = e2 / (1.0 + e2)
    return jnp.where(lane == 0, i1.astype(F32),
                     jnp.where(lane == 1, i2.astype(F32),
                               jnp.where(lane == 2, g1, jnp.where(lane == 3, g2, 0.0))))


def _store_packed(hn, h_ref):
    tm, d = hn.shape
    wpt = d // (2 * LANES)
    bits = pltpu.bitcast(hn.astype(BF16).astype(F32), U32)
    for c in range(wpt):
        lo = bits[:, c * LANES:(c + 1) * LANES] >> 16
        hi = bits[:, (c + wpt) * LANES:(c + wpt + 1) * LANES] & jnp.uint32(0xFFFF0000)
        h_ref[pl.ds(c, tm, stride=wpt), :] = lo | hi


def _finish(xn, g_ref, r_ref, xo_ref, h_ref, route_ref, n_experts):
    xo_ref[...] = xn
    hn = _rms(xn, g_ref[...])
    if route_ref is None:
        h_ref[...] = hn.astype(h_ref.dtype)
    else:
        _store_packed(hn, h_ref)
        route_ref[...] = _route(hn, r_ref, n_experts)


def _out_kernel(*refs, n_experts):
    if n_experts:
        ya_ref, yb_ref, w_ref, x_ref, g_ref, r_ref, xo_ref, h_ref, route_ref, wb_ref = refs
    else:
        ya_ref, yb_ref, w_ref, x_ref, g_ref, xo_ref, h_ref, wb_ref = refs
        r_ref = route_ref = None

    @pl.when(pl.program_id(0) == 0)
    def _():
        wb_ref[...] = w_ref[...].astype(BF16)

    wa = ya_ref.shape[1]
    acc = jnp.dot(ya_ref[...], wb_ref[:wa, :], preferred_element_type=F32)
    acc += jnp.dot(yb_ref[...], wb_ref[wa:, :], preferred_element_type=F32)
    _finish(x_ref[...] + acc, g_ref, r_ref, xo_ref, h_ref, route_ref, n_experts)


def _pad_router(router):
    d, e = router.shape
    return jnp.zeros((d, LANES), F32).at[:, :e].set(router)


def _out_proj(ya, yb, w_out, layer, x, g, router):
    n, d = x.shape
    tm = _pick(n, (256, 128))
    row = lambda w: pl.BlockSpec((tm, w), lambda i: (i, 0))
    in_specs = [row(ya.shape[1]), row(yb.shape[1]),
                pl.BlockSpec((None, d, d), lambda i: (layer, 0, 0), pipeline_mode=pl.Buffered(1)),
                row(d), pl.BlockSpec((1, d), lambda i: (0, 0))]
    args = [ya, yb, w_out, x, g.reshape(1, d)]
    n_experts = 0
    if router is None:
        out_specs = [row(d), row(d)]
        out_shape = [jax.ShapeDtypeStruct((n, d), F32), jax.ShapeDtypeStruct((n, d), BF16)]
    else:
        n_experts = router.shape[1]
        wpt = d // (2 * LANES)
        in_specs.append(pl.BlockSpec((d, LANES), lambda i: (0, 0)))
        args.append(_pad_router(router))
        out_specs = [row(d), pl.BlockSpec((tm * wpt, LANES), lambda i: (i, 0)), row(LANES)]
        out_shape = [jax.ShapeDtypeStruct((n, d), F32), jax.ShapeDtypeStruct((n * wpt, LANES), U32),
                     jax.ShapeDtypeStruct((n, LANES), F32)]
    return pl.pallas_call(
        functools.partial(_out_kernel, n_experts=n_experts),
        grid=(n // tm,),
        in_specs=in_specs,
        out_specs=out_specs,
        out_shape=out_shape,
        scratch_shapes=[pltpu.VMEM((d, d), BF16)],
        compiler_params=_params("arbitrary"),
        name="out_proj",
    )(*args)


def _up_kernel(te_ref, first_ref, act_ref, a_ref, w1_ref, w3_ref, o_ref, w1b_ref, w3b_ref):
    i = pl.program_id(1)

    @pl.when(first_ref[i] == 1)
    def _():
        w1b_ref[...] = w1_ref[...].astype(BF16)
        w3b_ref[...] = w3_ref[...].astype(BF16)

    @pl.when(act_ref[i] == 1)
    def _():
        for rows in _chunks(a_ref.shape[0]):
            a = a_ref[rows, :]
            h1 = jnp.dot(a, w1b_ref[...], preferred_element_type=F32)
            h3 = jnp.dot(a, w3b_ref[...], preferred_element_type=F32)
            o_ref[rows, :] = (_silu(h1) * h3).astype(o_ref.dtype)

    @pl.when(act_ref[i] == 0)
    def _():
        o_ref[...] = jnp.zeros_like(o_ref)


def _up(a, w1, w3, layer, plan, tm):
    r, d = a.shape
    f = w1.shape[3]
    tn = _pick(f, (512, 256, 128))
    w_spec = pl.BlockSpec((None, None, d, tn), lambda j, i, te, first, act: (layer, te[i], 0, j))
    grid_spec = pltpu.PrefetchScalarGridSpec(
        num_scalar_prefetch=3,
        grid=(f // tn, r // tm),
        in_specs=[pl.BlockSpec((tm, d), lambda j, i, te, first, act: (i, 0)), w_spec, w_spec],
        out_specs=pl.BlockSpec((tm, tn), lambda j, i, te, first, act: (i, j)),
        scratch_shapes=[pltpu.VMEM((d, tn), BF16)] * 2,
    )
    return pl.pallas_call(
        _up_kernel,
        grid_spec=grid_spec,
        out_shape=jax.ShapeDtypeStruct((r, f), BF16),
        compiler_params=_params("arbitrary", "arbitrary"),
        name="swiglu_up",
    )(*plan, a, w1, w3)


def _down_kernel(te_ref, first_ref, act_ref, a_ref, w_ref, *rest, residual):
    if residual:
        x_ref, o_ref, wb_ref = rest
    else:
        o_ref, wb_ref = rest
    i = pl.program_id(1)

    @pl.when(first_ref[i] == 1)
    def _():
        wb_ref[...] = w_ref[...].astype(BF16)

    @pl.when(act_ref[i] == 1)
    def _():
        acc = jnp.dot(a_ref[...], wb_ref[...], preferred_element_type=F32)
        o_ref[...] = x_ref[...] + acc if residual else acc

    @pl.when(act_ref[i] == 0)
    def _():
        o_ref[...] = jnp.zeros_like(o_ref)


def _down(a, w2, layer, plan, tm, x=None):
    r, f = a.shape
    d = w2.shape[3]
    tn = _pick(d, (512, 256, 128))
    blk = pl.BlockSpec((tm, tn), lambda j, i, te, first, act: (i, j))
    in_specs = [pl.BlockSpec((tm, f), lambda j, i, te, first, act: (i, 0)),
                pl.BlockSpec((None, None, f, tn), lambda j, i, te, first, act: (layer, te[i], 0, j))]
    args = [a, w2]
    if x is not None:
        in_specs.append(blk)
        args.append(x)
    grid_spec = pltpu.PrefetchScalarGridSpec(
        num_scalar_prefetch=3,
        grid=(d // tn, r // tm),
        in_specs=in_specs,
        out_specs=blk,
        scratch_shapes=[pltpu.VMEM((f, tn), BF16)],
    )
    return pl.pallas_call(
        functools.partial(_down_kernel, residual=x is not None),
        grid_spec=grid_spec,
        out_shape=jax.ShapeDtypeStruct((r, d), F32),
        compiler_params=_params("arbitrary", "arbitrary"),
        name="swiglu_down",
    )(*plan, *args)


def _single_group_plan(n_tiles):
    te = jnp.zeros((n_tiles,), jnp.int32)
    first = jnp.zeros((n_tiles,), jnp.int32).at[0].set(1)
    act = jnp.ones((n_tiles,), jnp.int32)
    return te, first, act


def _gather_kernel(src_ref, h_hbm, o_ref, buf_ref, sem, *, tg, wpt):
    i = pl.program_id(0)
    n_blocks = pl.num_programs(0)

    def row_copy(block, slot, r):
        tok = src_ref[block * tg + r]
        return pltpu.make_async_copy(h_hbm.at[pl.ds(tok * wpt, wpt)],
                                     buf_ref.at[pl.ds((slot * tg + r) * wpt, wpt)], sem.at[slot])

    def start_block(block, slot):
        def start(r, c):
            row_copy(block, slot, r).start()
            return c
        lax.fori_loop(0, tg, start, 0, unroll=8)

    @pl.when(i == 0)
    def _():
        start_block(0, 0)

    @pl.when(i + 1 < n_blocks)
    def _():
        start_block(i + 1, (i + 1) % 2)

    slot = i % 2

    def wait(r, c):
        row_copy(i, slot, r).wait()
        return c
    lax.fori_loop(0, tg, wait, 0, unroll=8)

    for c in range(wpt):
        w = buf_ref[pl.ds(slot * tg * wpt + c, tg, stride=wpt), :]
        o_ref[:, c * LANES:(c + 1) * LANES] = pltpu.bitcast(w << 16, F32).astype(o_ref.dtype)
        o_ref[:, (c + wpt) * LANES:(c + wpt + 1) * LANES] = (
            pltpu.bitcast(w & jnp.uint32(0xFFFF0000), F32).astype(o_ref.dtype))


def _gather_rows(hp, src, d):
    rp = src.shape[0]
    wpt = d // (2 * LANES)
    tg = _pick(rp, (256, 128))
    grid_spec = pltpu.PrefetchScalarGridSpec(
        num_scalar_prefetch=1,
        grid=(rp // tg,),
        in_specs=[pl.BlockSpec(memory_space=pl.ANY)],
        out_specs=pl.BlockSpec((tg, d), lambda i, src: (i, 0)),
        scratch_shapes=[pltpu.VMEM((2 * tg * wpt, LANES), U32), pltpu.SemaphoreType.DMA((2,))],
    )
    return pl.pallas_call(
        functools.partial(_gather_kernel, tg=tg, wpt=wpt),
        grid_spec=grid_spec,
        out_shape=jax.ShapeDtypeStruct((rp, d), BF16),
        compiler_params=_params("arbitrary"),
        name="moe_gather",
    )(src, hp)


def _combine_kernel(pos_ref, x_ref, route_ref, ys_hbm, g_ref, xo_ref, h_ref, buf_ref, sem, *, tt):
    base = pl.program_id(0) * tt

    def row_copy(r, k):
        return pltpu.make_async_copy(ys_hbm.at[pl.ds(pos_ref[TOP_K * (base + r) + k], 1)],
                                     buf_ref.at[k, pl.ds(r, 1)], sem.at[k])

    def start(r, c):
        for k in range(TOP_K):
            row_copy(r, k).start()
        return c

    def wait(r, c):
        for k in range(TOP_K):
            row_copy(r, k).wait()
        return c

    lax.fori_loop(0, tt, start, 0, unroll=4)
    lax.fori_loop(0, tt, wait, 0, unroll=4)
    route = route_ref[...]
    xn = x_ref[...] + route[:, 2:3] * buf_ref[0] + route[:, 3:4] * buf_ref[1]
    _finish(xn, g_ref, None, xo_ref, h_ref, None, 0)


def _combine(x, route, ys, pos, g, h_dtype):
    n, d = x.shape
    tt = _pick(n, (256, 128))
    row = pl.BlockSpec((tt, d), lambda i, pos: (i, 0))
    grid_spec = pltpu.PrefetchScalarGridSpec(
        num_scalar_prefetch=1,
        grid=(n // tt,),
        in_specs=[row, pl.BlockSpec((tt, LANES), lambda i, pos: (i, 0)),
                  pl.BlockSpec(memory_space=pl.ANY),
                  pl.BlockSpec((1, d), lambda i, pos: (0, 0))],
        out_specs=[row, row],
        scratch_shapes=[pltpu.VMEM((TOP_K, tt, d), F32), pltpu.SemaphoreType.DMA((TOP_K,))],
    )
    return pl.pallas_call(
        functools.partial(_combine_kernel, tt=tt),
        grid_spec=grid_spec,
        out_shape=[jax.ShapeDtypeStruct((n, d), F32), jax.ShapeDtypeStruct((n, d), h_dtype)],
        compiler_params=_params("arbitrary"),
        name="moe_combine",
    )(pos, x, route, ys, g.reshape(1, d))


def _tile_plan(gend, n_experts, rp, tm):
    start = jnp.arange(rp // tm, dtype=jnp.int32) * tm
    te = jnp.minimum(jnp.sum((start[:, None] >= gend[None, :]).astype(jnp.int32), axis=1), n_experts - 1)
    prev = jnp.concatenate([jnp.full((1,), -1, jnp.int32), te[:-1]])
    first = (te != prev).astype(jnp.int32)
    act = (start < gend[-1]).astype(jnp.int32)
    return te, first, act


def _dispatch_plan(route, n_experts, pad):
    n = route.shape[0]
    ef = route[:, :TOP_K].astype(jnp.int32).reshape(-1)
    onehot = (ef[:, None] == jnp.arange(n_experts, dtype=jnp.int32)[None, :]).astype(jnp.int32)
    csum = jnp.cumsum(onehot, axis=0)
    counts = csum[-1]
    rank = jnp.take_along_axis(csum, ef[:, None], axis=1)[:, 0] - 1
    padded = ((counts + pad - 1) // pad) * pad
    gend = jnp.cumsum(padded)
    gstart = gend - padded
    pos = (gstart[ef] + rank).astype(jnp.int32)
    rp = TOP_K * n + n_experts * pad
    src = jnp.zeros((rp,), jnp.int32).at[pos].set(jnp.arange(TOP_K * n, dtype=jnp.int32) // TOP_K)
    return pos, src, gend, rp


def kernel(x, positions, attn_norm, w_in, w_s, b_s, gmlp_norm, out_norm, w_out, ffn_norm, dense_w1, dense_w3,
           dense_w2, router, moe_w1, moe_w3, moe_w2, final_norm):
    batch, seq, d = x.shape
    depth = w_in.shape[0]
    n = batch * seq
    n_heads = d // HEAD_DIM
    n_heads_a = n_heads // 4
    n_heads_b = n_heads - n_heads_a
    width_b = n_heads_b * HEAD_DIM
    tm_up = _pick(n, (512, 256, 128))
    tm_down = _pick(n, (256, 128))

    xf = x.reshape(n, d)
    tabs = _rope_tables(positions)
    h = _norm(xf, attn_norm[0], BF16)
    for i in range(depth):
        moe = i % 2 == 1
        j = i // 2
        last = i == depth - 1
        next_g = final_norm if last else attn_norm[i + 1]
        next_dtype = F32 if last else BF16

        proj = _project(h, w_in, i, tabs, width_b)
        yb = _attention(proj, out_norm[i], batch, seq, n_heads_a, n_heads_b)
        ya = _gmlp(proj, w_s[i], b_s[i], gmlp_norm[i], out_norm[i], n_heads_a, n_heads_b)
        if not moe:
            xf, h2 = _out_proj(ya, yb, w_out, i, xf, ffn_norm[i], None)
            mid = _up(h2, dense_w1[:, None], dense_w3[:, None], j, _single_group_plan(n // tm_up), tm_up)
            xf = _down(mid, dense_w2[:, None], j, _single_group_plan(n // tm_down), tm_down, x=xf)
            h = _norm(xf, next_g, next_dtype)
        else:
            n_experts = router.shape[2]
            xf, h2p, route = _out_proj(ya, yb, w_out, i, xf, ffn_norm[i], router[j])
            pos, src, gend, rp = _dispatch_plan(route, n_experts, tm_up)
            xs = _gather_rows(h2p, src, d)
            mid = _up(xs, moe_w1, moe_w3, j, _tile_plan(gend, n_experts, rp, tm_up), tm_up)
            ys = _down(mid, moe_w2, j, _tile_plan(gend, n_experts, rp, tm_down), tm_down)
            xf, h = _combine(xf, route, ys, pos, next_g, next_dtype)
    return h.reshape(batch, seq, d)
```

```python
import functools
import math

import numpy as np
import jax
import jax.numpy as jnp
from jax import lax
from jax.experimental import pallas as pl
from jax.experimental.pallas import tpu as pltpu

HEAD_DIM = 128
CHUNK = 128
RADIUS = 64
DILATIONS = (16, 4, 1)
ROT_DIM = HEAD_DIM // 4
ROPE_THETA = 500000.0
TOP_K = 2
EPS = 1e-5
NEG = -1e30
LANES = 128
VMEM_LIMIT_BYTES = 56 * 1024 * 1024
DOWN_VMEM_BUDGET_BYTES = 48 * 1024 * 1024
ROW_CHUNK = 256

F32 = jnp.float32
BF16 = jnp.bfloat16
U32 = jnp.uint32


def _pick(n, prefs):
    for p in prefs:
        if n % p == 0:
            return p
    raise ValueError(f"no tile in {prefs} divides {n}")


def _params(*sem):
    return pltpu.CompilerParams(dimension_semantics=sem, vmem_limit_bytes=VMEM_LIMIT_BYTES)


def _rms(x, g):
    ms = jnp.mean(x * x, axis=-1, keepdims=True)
    return x * lax.rsqrt(ms + EPS) * g


def _gelu(x):
    return 0.5 * x * (1.0 + jnp.tanh(math.sqrt(2.0 / math.pi) * (x + 0.044715 * (x * x * x))))


def _silu(x):
    hx = 0.5 * x
    return hx + hx * jnp.tanh(hx)


def _chunks(tm):
    cr = ROW_CHUNK if tm % ROW_CHUNK == 0 else tm
    return [slice(c * cr, (c + 1) * cr) for c in range(tm // cr)]


def _rope_kernel(pos_ref, inv_ref, c_ref, s1_ref, s2_ref):
    ang = pos_ref[...].astype(F32) * inv_ref[...]
    lane = lax.broadcasted_iota(jnp.int32, ang.shape, 1)
    sn = jnp.sin(ang)
    c_ref[...] = jnp.cos(ang)
    s1_ref[...] = jnp.where(lane < ROT_DIM // 2, -sn, 0.0)
    s2_ref[...] = jnp.where(lane >= ROT_DIM // 2, sn, 0.0)


def _rope_tables(positions):
    n = positions.size
    half = ROT_DIM // 2
    inv = ROPE_THETA ** (-np.arange(0, ROT_DIM, 2, dtype=np.float32) / ROT_DIM)
    inv_lane = np.zeros((1, LANES), np.float32)
    inv_lane[0, :half] = inv
    inv_lane[0, half:ROT_DIM] = inv
    tm = _pick(n, (2048, 1024, 512, 256, 128))
    out = jax.ShapeDtypeStruct((n, LANES), F32)
    return pl.pallas_call(
        _rope_kernel,
        grid=(n // tm,),
        in_specs=[pl.BlockSpec((tm, 1), lambda i: (i, 0)),
                  pl.BlockSpec((1, LANES), lambda i: (0, 0))],
        out_specs=[pl.BlockSpec((tm, LANES), lambda i: (i, 0))] * 3,
        out_shape=[out, out, out],
        compiler_params=_params("parallel"),
        name="rope_tables",
    )(positions.reshape(n, 1), jnp.asarray(inv_lane))


def _norm_kernel(x_ref, g_ref, o_ref):
    o_ref[...] = _rms(x_ref[...], g_ref[...]).astype(o_ref.dtype)


def _norm(x, g, dtype):
    n, d = x.shape
    tm = _pick(n, (512, 256, 128))
    return pl.pallas_call(
        _norm_kernel,
        grid=(n // tm,),
        in_specs=[pl.BlockSpec((tm, d), lambda i: (i, 0)),
                  pl.BlockSpec((1, d), lambda i: (0, 0))],
        out_specs=pl.BlockSpec((tm, d), lambda i: (i, 0)),
        out_shape=jax.ShapeDtypeStruct((n, d), dtype),
        compiler_params=_params("parallel"),
        name="row_norm",
    )(x, g.reshape(1, d))


def _proj_kernel(a_ref, w_ref, *rest, heads_per_tile, q_scale, rotary):
    if rotary:
        c_ref, s1_ref, s2_ref, o_ref, wb_ref = rest
    else:
        o_ref, wb_ref = rest
    j = pl.program_id(0)

    @pl.when(pl.program_id(1) == 0)
    def _():
        wb_ref[...] = w_ref[...].astype(BF16)

    def emit(rope, scale):
        for rows in _chunks(a_ref.shape[0]):
            acc = jnp.dot(a_ref[rows, :], wb_ref[...], preferred_element_type=F32)
            if rope:
                c = c_ref[rows, :]
                s1 = s1_ref[rows, :]
                s2 = s2_ref[rows, :]
            for h in range(heads_per_tile):
                t = acc[:, h * LANES:(h + 1) * LANES]
                if rope:
                    lo = pltpu.roll(t, LANES - ROT_DIM // 2, 1)
                    hi = pltpu.roll(t, ROT_DIM // 2, 1)
                    t = (t * c + lo * s1 + hi * s2) * scale
                o_ref[h, rows, :] = t.astype(o_ref.dtype)

    if rotary:
        pl.when(j == 0)(lambda: emit(True, q_scale))
        pl.when(j == 1)(lambda: emit(True, 1.0))
        pl.when(j >= 2)(lambda: emit(False, 1.0))
    else:
        emit(False, 1.0)


def _project(h, w_in, layer, tabs, width_b):
    n, d = h.shape
    p = w_in.shape[2]
    tm = _pick(n, (512, 256, 128))
    q_scale = math.log2(math.e) * HEAD_DIM ** -0.5

    def call(tn, col0, n_tiles, rotary, dtype, name):
        assert col0 % tn == 0
        hpt = tn // LANES
        in_specs = [pl.BlockSpec((tm, d), lambda j, i: (i, 0)),
                    pl.BlockSpec((None, d, tn), lambda j, i: (layer, 0, col0 // tn + j))]
        args = [h, w_in]
        if rotary:
            in_specs += [pl.BlockSpec((tm, LANES), lambda j, i: (i, 0))] * 3
            args += list(tabs)
        return pl.pallas_call(
            functools.partial(_proj_kernel, heads_per_tile=hpt, q_scale=q_scale, rotary=rotary),
            grid=(n_tiles, n // tm),
            in_specs=in_specs,
            out_specs=pl.BlockSpec((hpt, tm, LANES), lambda j, i: (j, i, 0)),
            out_shape=jax.ShapeDtypeStruct((n_tiles * hpt, n, LANES), dtype),
            scratch_shapes=[pltpu.VMEM((d, tn), BF16)],
            compiler_params=_params("arbitrary", "arbitrary"),
            name=name,
        )(*args)

    qkv = call(width_b, 0, 3, True, F32, "in_proj_qkv")
    tn_uv = _pick(math.gcd(3 * width_b, p - 3 * width_b), (512, 256, 128))
    uv = call(tn_uv, 3 * width_b, (p - 3 * width_b) // tn_uv, False, BF16, "in_proj_uv")
    return qkv, uv


def _attn_kernel(q_ref, k_ref, v_ref, g_ref, o_ref, qd_ref, kd_ref, vd_ref, bias_ref, acc_ref, m_ref, l_ref, *,
                 seq, tq, unroll):
    win = tq + 2 * RADIUS
    copy_rows = 256
    qq = lax.broadcasted_iota(jnp.int32, (tq, win), 0)
    kk = lax.broadcasted_iota(jnp.int32, (tq, win), 1)
    for case in range(3):
        bias_ref[case] = jnp.where(jnp.abs(kk - qq - case * RADIUS) <= RADIUS, 0.0, NEG)
    ones = jnp.ones((win, LANES), BF16)

    for d in DILATIONS:
        sub_len = seq // d
        nt = sub_len // tq
        first = d == DILATIONS[0]
        last = d == DILATIONS[-1]

        def deinterleave(c, carry, d=d, sub_len=sub_len):
            per = sub_len // copy_rows
            r = c // per
            m0 = (c % per) * copy_rows
            src = pl.ds(r + d * m0, copy_rows, stride=d) if d > 1 else pl.ds(pl.multiple_of(m0, copy_rows), copy_rows)
            dst = pl.ds(pl.multiple_of(c * copy_rows, copy_rows), copy_rows)
            qd_ref[dst, :] = q_ref[src, :].astype(BF16)
            kd_ref[dst, :] = k_ref[src, :].astype(BF16)
            vd_ref[dst, :] = v_ref[src, :].astype(BF16)
            return carry

        lax.fori_loop(0, seq // copy_rows, deinterleave, 0)

        def tile(t, d=d, sub_len=sub_len, nt=nt):
            r = t // nt
            ti = t % nt
            q0 = ti * tq
            k0 = jnp.clip(q0 - RADIUS, 0, sub_len - win)
            case = jnp.where(ti == 0, 0, jnp.where(ti == nt - 1, 2, 1))
            kbase = pl.multiple_of(r * sub_len + k0, RADIUS)
            q = qd_ref[pl.ds(pl.multiple_of(t * tq, tq), tq), :]
            k = kd_ref[pl.ds(kbase, win), :]
            v = vd_ref[pl.ds(kbase, win), :]
            s = lax.dot_general(q, k, (((1,), (1,)), ((), ())), preferred_element_type=F32) + bias_ref[case]
            m_t = jnp.max(s, axis=-1, keepdims=True)
            p = jnp.exp2(s - m_t).astype(BF16)
            pv = jnp.dot(p, jnp.concatenate([v, ones], axis=1), preferred_element_type=F32)
            if d == 1:
                rows = pl.ds(pl.multiple_of(q0, tq), tq)
            else:
                rows = pl.ds(r + d * q0, tq, stride=d)
            return rows, jnp.broadcast_to(m_t, (tq, LANES)), pv[:, LANES:], pv[:, :LANES]

        def merge(rows, m_t, l_t, acc_t, first=first, last=last):
            if first:
                m_new, l_new, acc_new = m_t, l_t, acc_t
            else:
                m_prev = m_ref[rows, :]
                m_new = jnp.maximum(m_prev, m_t)
                a = jnp.exp2(m_prev - m_new)
                b = jnp.exp2(m_t - m_new)
                l_new = a * l_ref[rows, :] + b * l_t
                acc_new = a * acc_ref[rows, :] + b * acc_t
            if last:
                o_ref[rows, :] = _rms(acc_new / l_new, g_ref[...]).astype(o_ref.dtype)
            else:
                m_ref[rows, :] = m_new
                l_ref[rows, :] = l_new
                acc_ref[rows, :] = acc_new

        def body(it, carry):
            parts = [tile(it * unroll + u) for u in range(unroll)]
            for part in parts:
                merge(*part)
            return carry

        lax.fori_loop(0, seq // (tq * unroll), body, 0)


def _attention(proj, g_out, batch, seq, n_heads_a, n_heads_b):
    tq = 128
    unroll = 8
    win = tq + 2 * RADIUS
    assert seq % (max(DILATIONS) * tq) == 0 and (seq // tq) % unroll == 0 and seq // max(DILATIONS) >= max(win, 256)
    n = batch * seq
    kern = functools.partial(_attn_kernel, seq=seq, tq=tq, unroll=unroll)
    blk = lambda off: pl.BlockSpec((None, seq, LANES), lambda b, h: (off + h, b, 0))
    return pl.pallas_call(
        kern,
        grid=(batch, n_heads_b),
        in_specs=[blk(0), blk(n_heads_b), blk(2 * n_heads_b),
                  pl.BlockSpec((None, 1, LANES), lambda b, h: (n_heads_a + h, 0, 0))],
        out_specs=pl.BlockSpec((seq, LANES), lambda b, h: (b, h)),
        out_shape=jax.ShapeDtypeStruct((n, n_heads_b * LANES), BF16),
        scratch_shapes=[pltpu.VMEM((seq, LANES), BF16)] * 3
        + [pltpu.VMEM((3, tq, win), F32)]
        + [pltpu.VMEM((seq, LANES), F32)] * 3,
        compiler_params=_params("parallel", "parallel"),
        name="dilated_attention",
    )(proj, proj, proj, g_out.reshape(-1, 1, LANES))


def _gmlp_kernel(u_ref, v_ref, ws_ref, bs_ref, gv_ref, go_ref, o_ref, *, n_heads_a, n_chunks):
    for h in range(n_heads_a):
        w = ws_ref[h].astype(BF16)
        b = bs_ref[h]
        gv = gv_ref[h]
        go = go_ref[h]
        for c in range(n_chunks):
            rows = slice(c * CHUNK, (c + 1) * CHUNK)
            u = _gelu(u_ref[h, rows, :].astype(F32))
            v = _rms(_gelu(v_ref[h, rows, :].astype(F32)), gv)
            sv = jnp.dot(w, v.astype(BF16), preferred_element_type=F32) + b
            o_ref[rows, h * LANES:(h + 1) * LANES] = _rms(u * sv, go).astype(o_ref.dtype)


def _gmlp(uv, w_s, b_s, g_v, g_out, n_heads_a):
    n = uv.shape[1]
    tt = _pick(n, (512, 256, 128))
    kern = functools.partial(_gmlp_kernel, n_heads_a=n_heads_a, n_chunks=tt // CHUNK)
    full = lambda shp: pl.BlockSpec(shp, lambda i: (0,) * len(shp))
    return pl.pallas_call(
        kern,
        grid=(n // tt,),
        in_specs=[pl.BlockSpec((n_heads_a, tt, LANES), lambda i: (0, i, 0)),
                  pl.BlockSpec((n_heads_a, tt, LANES), lambda i: (1, i, 0)),
                  full((n_heads_a, CHUNK, CHUNK)),
                  full((n_heads_a, CHUNK, 1)),
                  full((n_heads_a, 1, LANES)),
                  full((n_heads_a, 1, LANES))],
        out_specs=pl.BlockSpec((tt, n_heads_a * LANES), lambda i: (i, 0)),
        out_shape=jax.ShapeDtypeStruct((n, n_heads_a * LANES), BF16),
        compiler_params=_params("parallel"),
        name="gmlp_gating",
    )(uv, uv, w_s, b_s.reshape(n_heads_a, CHUNK, 1), g_v.reshape(n_heads_a, 1, LANES),
      g_out[:n_heads_a].reshape(n_heads_a, 1, LANES))


def _route(hn, r_ref, n_experts):
    r = r_ref[...]
    h_hi = hn.astype(BF16)
    h_lo = (hn - h_hi.astype(F32)).astype(BF16)
    r_hi = r.astype(BF16)
    r_lo = (r - r_hi.astype(F32)).astype(BF16)
    logits = (jnp.dot(h_hi, r_hi, preferred_element_type=F32) + jnp.dot(h_lo, r_hi, preferred_element_type=F32)
              + jnp.dot(h_hi, r_lo, preferred_element_type=F32))
    lane = lax.broadcasted_iota(jnp.int32, logits.shape, 1)
    lg = jnp.where(lane < n_experts, logits, -jnp.inf)
    v1 = jnp.max(lg, axis=-1, keepdims=True)
    i1 = jnp.min(jnp.where(lg == v1, lane, LANES), axis=-1, keepdims=True)
    lg2 = jnp.where(lane == i1, -jnp.inf, lg)
    v2 = jnp.max(lg2, axis=-1, keepdims=True)
    i2 = jnp.min(jnp.where(lg2 == v2, lane, LANES), axis=-1, keepdims=True)
    e2 = jnp.exp(v2 - v1)
    g1 = 1.0 / (1.0 + e2)
    g2 = e2 / (1.0 + e2)
    return jnp.where(lane == 0, i1.astype(F32),
                     jnp.where(lane == 1, i2.astype(F32),
                               jnp.where(lane == 2, g1, jnp.where(lane == 3, g2, 0.0))))


def _store_packed(hn, h_ref):
    tm, d = hn.shape
    wpt = d // (2 * LANES)
    bits = pltpu.bitcast(hn.astype(BF16).astype(F32), U32)
    for c in range(wpt):
        lo = bits[:, c * LANES:(c + 1) * LANES] >> 16
        hi = bits[:, (c + wpt) * LANES:(c + wpt + 1) * LANES] & jnp.uint32(0xFFFF0000)
        h_ref[pl.ds(c, tm, stride=wpt), :] = lo | hi


def _finish(xn, g_ref, r_ref, xo_ref, h_ref, route_ref, n_experts):
    xo_ref[...] = xn
    hn = _rms(xn, g_ref[...])
    if route_ref is None:
        h_ref[...] = hn.astype(h_ref.dtype)
    else:
        _store_packed(hn, h_ref)
        route_ref[...] = _route(hn, r_ref, n_experts)


def _out_kernel(*refs, n_experts):
    if n_experts:
        ya_ref, yb_ref, w_ref, x_ref, g_ref, r_ref, xo_ref, h_ref, route_ref, wb_ref = refs
    else:
        ya_ref, yb_ref, w_ref, x_ref, g_ref, xo_ref, h_ref, wb_ref = refs
        r_ref = route_ref = None

    @pl.when(pl.program_id(0) == 0)
    def _():
        wb_ref[...] = w_ref[...].astype(BF16)

    wa = ya_ref.shape[1]
    acc = jnp.dot(ya_ref[...], wb_ref[:wa, :], preferred_element_type=F32)
    acc += jnp.dot(yb_ref[...], wb_ref[wa:, :], preferred_element_type=F32)
    _finish(x_ref[...] + acc, g_ref, r_ref, xo_ref, h_ref, route_ref, n_experts)


def _pad_router(router):
    d, e = router.shape
    return jnp.zeros((d, LANES), F32).at[:, :e].set(router)


def _out_proj(ya, yb, w_out, layer, x, g, router):
    n, d = x.shape
    tm = _pick(n, (256, 128))
    row = lambda w: pl.BlockSpec((tm, w), lambda i: (i, 0))
    in_specs = [row(ya.shape[1]), row(yb.shape[1]),
                pl.BlockSpec((None, d, d), lambda i: (layer, 0, 0), pipeline_mode=pl.Buffered(1)),
                row(d), pl.BlockSpec((1, d), lambda i: (0, 0))]
    args = [ya, yb, w_out, x, g.reshape(1, d)]
    n_experts = 0
    if router is None:
        out_specs = [row(d), row(d)]
        out_shape = [jax.ShapeDtypeStruct((n, d), F32), jax.ShapeDtypeStruct((n, d), BF16)]
    else:
        n_experts = router.shape[1]
        wpt = d // (2 * LANES)
        in_specs.append(pl.BlockSpec((d, LANES), lambda i: (0, 0)))
        args.append(_pad_router(router))
        out_specs = [row(d), pl.BlockSpec((tm * wpt, LANES), lambda i: (i, 0)), row(LANES)]
        out_shape = [jax.ShapeDtypeStruct((n, d), F32), jax.ShapeDtypeStruct((n * wpt, LANES), U32),
                     jax.ShapeDtypeStruct((n, LANES), F32)]
    return pl.pallas_call(
        functools.partial(_out_kernel, n_experts=n_experts),
        grid=(n // tm,),
        in_specs=in_specs,
        out_specs=out_specs,
        out_shape=out_shape,
        scratch_shapes=[pltpu.VMEM((d, d), BF16)],
        compiler_params=_params("arbitrary"),
        name="out_proj",
    )(*args)


def _up_kernel(te_ref, first_ref, act_ref, a_ref, w1_ref, w3_ref, o_ref, w1b_ref, w3b_ref):
    i = pl.program_id(1)

    @pl.when(first_ref[i] == 1)
    def _():
        w1b_ref[...] = w1_ref[...].astype(BF16)
        w3b_ref[...] = w3_ref[...].astype(BF16)

    @pl.when(act_ref[i] == 1)
    def _():
        for rows in _chunks(a_ref.shape[0]):
            a = a_ref[rows, :]
            h1 = jnp.dot(a, w1b_ref[...], preferred_element_type=F32)
            h3 = jnp.dot(a, w3b_ref[...], preferred_element_type=F32)
            o_ref[rows, :] = (_silu(h1) * h3).astype(o_ref.dtype)

    @pl.when(act_ref[i] == 0)
    def _():
        o_ref[...] = jnp.zeros_like(o_ref)


def _up(a, w1, w3, layer, plan, tm):
    r, d = a.shape
    f = w1.shape[3]
    tn = _pick(f, (1024, 512, 256, 128))
    w_spec = pl.BlockSpec((None, None, d, tn), lambda j, i, te, first, act: (layer, te[i], 0, j))
    grid_spec = pltpu.PrefetchScalarGridSpec(
        num_scalar_prefetch=3,
        grid=(f // tn, r // tm),
        in_specs=[pl.BlockSpec((tm, d), lambda j, i, te, first, act: (i, 0)), w_spec, w_spec],
        out_specs=pl.BlockSpec((tm, tn), lambda j, i, te, first, act: (i, j)),
        scratch_shapes=[pltpu.VMEM((d, tn), BF16)] * 2,
    )
    return pl.pallas_call(
        _up_kernel,
        grid_spec=grid_spec,
        out_shape=jax.ShapeDtypeStruct((r, f), BF16),
        compiler_params=_params("arbitrary", "arbitrary"),
        name="swiglu_up",
    )(*plan, a, w1, w3)


def _down_kernel(te_ref, first_ref, act_ref, a_ref, w_ref, *rest, residual):
    if residual:
        x_ref, o_ref, wb_ref = rest
    else:
        o_ref, wb_ref = rest
    i = pl.program_id(1)

    @pl.when(first_ref[i] == 1)
    def _():
        wb_ref[...] = w_ref[...].astype(BF16)

    @pl.when(act_ref[i] == 1)
    def _():
        acc = jnp.dot(a_ref[...], wb_ref[...], preferred_element_type=F32)
        o_ref[...] = x_ref[...] + acc if residual else acc

    @pl.when(act_ref[i] == 0)
    def _():
        o_ref[...] = jnp.zeros_like(o_ref)


def _down(a, w2, layer, plan, tm, x=None):
    r, f = a.shape
    d = w2.shape[3]

    def vmem_bytes(tn, w_bufs):
        io_blocks = 2 * (tm * f * 2 + tm * tn * 4 * (2 if x is not None else 1))
        return f * tn * (4 * w_bufs + 2) + io_blocks

    tn, w_bufs = next((t, b) for t in (1024, 512, 256, 128) for b in (2, 1)
                      if d % t == 0 and vmem_bytes(t, b) <= DOWN_VMEM_BUDGET_BYTES)
    w_mode = {} if w_bufs == 2 else {"pipeline_mode": pl.Buffered(1)}
    blk = pl.BlockSpec((tm, tn), lambda j, i, te, first, act: (i, j))
    in_specs = [pl.BlockSpec((tm, f), lambda j, i, te, first, act: (i, 0)),
                pl.BlockSpec((None, None, f, tn), lambda j, i, te, first, act: (layer, te[i], 0, j), **w_mode)]
    args = [a, w2]
    if x is not None:
        in_specs.append(blk)
        args.append(x)
    grid_spec = pltpu.PrefetchScalarGridSpec(
        num_scalar_prefetch=3,
        grid=(d // tn, r // tm),
        in_specs=in_specs,
        out_specs=blk,
        scratch_shapes=[pltpu.VMEM((f, tn), BF16)],
    )
    return pl.pallas_call(
        functools.partial(_down_kernel, residual=x is not None),
        grid_spec=grid_spec,
        out_shape=jax.ShapeDtypeStruct((r, d), F32),
        compiler_params=_params("arbitrary", "arbitrary"),
        name="swiglu_down",
    )(*plan, *args)


def _single_group_plan(n_tiles):
    te = jnp.zeros((n_tiles,), jnp.int32)
    first = jnp.zeros((n_tiles,), jnp.int32).at[0].set(1)
    act = jnp.ones((n_tiles,), jnp.int32)
    return te, first, act


def _gather_kernel(src_ref, h_hbm, o_ref, buf_ref, sem, *, tg, wpt):
    i = pl.program_id(0)
    n_blocks = pl.num_programs(0)

    def row_copy(block, slot, r):
        tok = src_ref[block * tg + r]
        return pltpu.make_async_copy(h_hbm.at[pl.ds(tok * wpt, wpt)],
                                     buf_ref.at[pl.ds((slot * tg + r) * wpt, wpt)], sem.at[slot])

    def start_block(block, slot):
        def start(r, c):
            row_copy(block, slot, r).start()
            return c
        lax.fori_loop(0, tg, start, 0, unroll=8)

    @pl.when(i == 0)
    def _():
        start_block(0, 0)

    @pl.when(i + 1 < n_blocks)
    def _():
        start_block(i + 1, (i + 1) % 2)

    slot = i % 2

    def wait(r, c):
        row_copy(i, slot, r).wait()
        return c
    lax.fori_loop(0, tg, wait, 0, unroll=8)

    for c in range(wpt):
        w = buf_ref[pl.ds(slot * tg * wpt + c, tg, stride=wpt), :]
        o_ref[:, c * LANES:(c + 1) * LANES] = pltpu.bitcast(w << 16, F32).astype(o_ref.dtype)
        o_ref[:, (c + wpt) * LANES:(c + wpt + 1) * LANES] = (
            pltpu.bitcast(w & jnp.uint32(0xFFFF0000), F32).astype(o_ref.dtype))


def _gather_rows(hp, src, d):
    rp = src.shape[0]
    wpt = d // (2 * LANES)
    tg = _pick(rp, (256, 128))
    grid_spec = pltpu.PrefetchScalarGridSpec(
        num_scalar_prefetch=1,
        grid=(rp // tg,),
        in_specs=[pl.BlockSpec(memory_space=pl.ANY)],
        out_specs=pl.BlockSpec((tg, d), lambda i, src: (i, 0)),
        scratch_shapes=[pltpu.VMEM((2 * tg * wpt, LANES), U32), pltpu.SemaphoreType.DMA((2,))],
    )
    return pl.pallas_call(
        functools.partial(_gather_kernel, tg=tg, wpt=wpt),
        grid_spec=grid_spec,
        out_shape=jax.ShapeDtypeStruct((rp, d), BF16),
        compiler_params=_params("arbitrary"),
        name="moe_gather",
    )(src, hp)


def _combine_kernel(pos_ref, x_ref, route_ref, ys_hbm, g_ref, xo_ref, h_ref, buf_ref, sem, *, tt):
    base = pl.program_id(0) * tt

    def row_copy(r, k):
        return pltpu.make_async_copy(ys_hbm.at[pl.ds(pos_ref[TOP_K * (base + r) + k], 1)],
                                     buf_ref.at[k, pl.ds(r, 1)], sem.at[k])

    def start(r, c):
        for k in range(TOP_K):
            row_copy(r, k).start()
        return c

    def wait(r, c):
        for k in range(TOP_K):
            row_copy(r, k).wait()
        return c

    lax.fori_loop(0, tt, start, 0, unroll=4)
    lax.fori_loop(0, tt, wait, 0, unroll=4)
    route = route_ref[...]
    xn = x_ref[...] + route[:, 2:3] * buf_ref[0] + route[:, 3:4] * buf_ref[1]
    _finish(xn, g_ref, None, xo_ref, h_ref, None, 0)


def _combine(x, route, ys, pos, g, h_dtype):
    n, d = x.shape
    tt = _pick(n, (256, 128))
    row = pl.BlockSpec((tt, d), lambda i, pos: (i, 0))
    grid_spec = pltpu.PrefetchScalarGridSpec(
        num_scalar_prefetch=1,
        grid=(n // tt,),
        in_specs=[row, pl.BlockSpec((tt, LANES), lambda i, pos: (i, 0)),
                  pl.BlockSpec(memory_space=pl.ANY),
                  pl.BlockSpec((1, d), lambda i, pos: (0, 0))],
        out_specs=[row, row],
        scratch_shapes=[pltpu.VMEM((TOP_K, tt, d), F32), pltpu.SemaphoreType.DMA((TOP_K,))],
    )
    return pl.pallas_call(
        functools.partial(_combine_kernel, tt=tt),
        grid_spec=grid_spec,
        out_shape=[jax.ShapeDtypeStruct((n, d), F32), jax.ShapeDtypeStruct((n, d), h_dtype)],
        compiler_params=_params("arbitrary"),
        name="moe_combine",
    )(pos, x, route, ys, g.reshape(1, d))


def _tile_plan(gend, n_experts, rp, tm):
    start = jnp.arange(rp // tm, dtype=jnp.int32) * tm
    te = jnp.minimum(jnp.sum((start[:, None] >= gend[None, :]).astype(jnp.int32), axis=1), n_experts - 1)
    prev = jnp.concatenate([jnp.full((1,), -1, jnp.int32), te[:-1]])
    first = (te != prev).astype(jnp.int32)
    act = (start < gend[-1]).astype(jnp.int32)
    return te, first, act


def _dispatch_plan(route, n_experts, pad):
    n = route.shape[0]
    ef = route[:, :TOP_K].astype(jnp.int32).reshape(-1)
    onehot = (ef[:, None] == jnp.arange(n_experts, dtype=jnp.int32)[None, :]).astype(jnp.int32)
    csum = jnp.cumsum(onehot, axis=0)
    counts = csum[-1]
    rank = jnp.take_along_axis(csum, ef[:, None], axis=1)[:, 0] - 1
    padded = ((counts + pad - 1) // pad) * pad
    gend = jnp.cumsum(padded)
    gstart = gend - padded
    pos = (gstart[ef] + rank).astype(jnp.int32)
    rp = TOP_K * n + n_experts * pad
    src = jnp.zeros((rp,), jnp.int32).at[pos].set(jnp.arange(TOP_K * n, dtype=jnp.int32) // TOP_K)
    return pos, src, gend, rp


def kernel(x, positions, attn_norm, w_in, w_s, b_s, gmlp_norm, out_norm, w_out, ffn_norm, dense_w1, dense_w3,
           dense_w2, router, moe_w1, moe_w3, moe_w2, final_norm):
    batch, seq, d = x.shape
    depth = w_in.shape[0]
    n = batch * seq
    n_heads = d // HEAD_DIM
    n_heads_a = n_heads // 4
    n_heads_b = n_heads - n_heads_a
    width_b = n_heads_b * HEAD_DIM
    tm_up = _pick(n, (512, 256, 128))
    tm_down = _pick(n, (256, 128))

    xf = x.reshape(n, d)
    tabs = _rope_tables(positions)
    h = _norm(xf, attn_norm[0], BF16)
    for i in range(depth):
        moe = i % 2 == 1
        j = i // 2
        last = i == depth - 1
        next_g = final_norm if last else attn_norm[i + 1]
        next_dtype = F32 if last else BF16

        qkv, uv = _project(h, w_in, i, tabs, width_b)
        yb = _attention(qkv, out_norm[i], batch, seq, n_heads_a, n_heads_b)
        ya = _gmlp(uv, w_s[i], b_s[i], gmlp_norm[i], out_norm[i], n_heads_a)
        if not moe:
            xf, h2 = _out_proj(ya, yb, w_out, i, xf, ffn_norm[i], None)
            mid = _up(h2, dense_w1[:, None], dense_w3[:, None], j, _single_group_plan(n // tm_up), tm_up)
            xf = _down(mid, dense_w2[:, None], j, _single_group_plan(n // tm_down), tm_down, x=xf)
            h = _norm(xf, next_g, next_dtype)
        else:
            n_experts = router.shape[2]
            xf, h2p, route = _out_proj(ya, yb, w_out, i, xf, ffn_norm[i], router[j])
            pos, src, gend, rp = _dispatch_plan(route, n_experts, tm_up)
            xs = _gather_rows(h2p, src, d)
            mid = _up(xs, moe_w1, moe_w3, j, _tile_plan(gend, n_experts, rp, tm_up), tm_up)
            ys = _down(mid, moe_w2, j, _tile_plan(gend, n_experts, rp, tm_down), tm_down)
            xf, h = _combine(xf, route, ys, pos, next_g, next_dtype)
    return h.reshape(batch, seq, d)
```

```python
import functools
import math

import numpy as np
import jax
import jax.numpy as jnp
from jax import lax
from jax.experimental import pallas as pl
from jax.experimental.pallas import tpu as pltpu

HEAD_DIM = 128
CHUNK = 128
RADIUS = 64
DILATIONS = (16, 4, 1)
ROT_DIM = HEAD_DIM // 4
ROPE_THETA = 500000.0
TOP_K = 2
EPS = 1e-5
NEG = -1e30
LANES = 128
VMEM_LIMIT_BYTES = 56 * 1024 * 1024
DOWN_VMEM_BUDGET_BYTES = 48 * 1024 * 1024
COL_CHUNK = 256

F32 = jnp.float32
BF16 = jnp.bfloat16
U32 = jnp.uint32


def _pick(n, prefs):
    for p in prefs:
        if n % p == 0:
            return p
    raise ValueError(f"no tile in {prefs} divides {n}")


def _params(*sem):
    return pltpu.CompilerParams(dimension_semantics=sem, vmem_limit_bytes=VMEM_LIMIT_BYTES)


def _rms(x, g):
    ms = jnp.mean(x * x, axis=-1, keepdims=True)
    return x * lax.rsqrt(ms + EPS) * g


def _gelu(x):
    return 0.5 * x * (1.0 + jnp.tanh(math.sqrt(2.0 / math.pi) * (x + 0.044715 * (x * x * x))))


def _silu(x):
    hx = 0.5 * x
    return hx + hx * jnp.tanh(hx)


def _chunks(tn):
    cw = COL_CHUNK if tn % COL_CHUNK == 0 else LANES
    return [slice(c * cw, (c + 1) * cw) for c in range(tn // cw)]


def _rope_kernel(pos_ref, inv_ref, c_ref, s1_ref, s2_ref):
    ang = pos_ref[...].astype(F32) * inv_ref[...]
    lane = lax.broadcasted_iota(jnp.int32, ang.shape, 1)
    sn = jnp.sin(ang)
    c_ref[...] = jnp.cos(ang)
    s1_ref[...] = jnp.where(lane < ROT_DIM // 2, -sn, 0.0)
    s2_ref[...] = jnp.where(lane >= ROT_DIM // 2, sn, 0.0)


def _rope_tables(positions):
    n = positions.size
    half = ROT_DIM // 2
    inv = ROPE_THETA ** (-np.arange(0, ROT_DIM, 2, dtype=np.float32) / ROT_DIM)
    inv_lane = np.zeros((1, LANES), np.float32)
    inv_lane[0, :half] = inv
    inv_lane[0, half:ROT_DIM] = inv
    tm = _pick(n, (2048, 1024, 512, 256, 128))
    out = jax.ShapeDtypeStruct((n, LANES), F32)
    return pl.pallas_call(
        _rope_kernel,
        grid=(n // tm,),
        in_specs=[pl.BlockSpec((tm, 1), lambda i: (i, 0)),
                  pl.BlockSpec((1, LANES), lambda i: (0, 0))],
        out_specs=[pl.BlockSpec((tm, LANES), lambda i: (i, 0))] * 3,
        out_shape=[out, out, out],
        compiler_params=_params("parallel"),
        name="rope_tables",
    )(positions.reshape(n, 1), jnp.asarray(inv_lane))


def _norm_kernel(x_ref, g_ref, o_ref):
    o_ref[...] = _rms(x_ref[...], g_ref[...]).astype(o_ref.dtype)


def _norm(x, g, dtype):
    n, d = x.shape
    tm = _pick(n, (512, 256, 128))
    return pl.pallas_call(
        _norm_kernel,
        grid=(n // tm,),
        in_specs=[pl.BlockSpec((tm, d), lambda i: (i, 0)),
                  pl.BlockSpec((1, d), lambda i: (0, 0))],
        out_specs=pl.BlockSpec((tm, d), lambda i: (i, 0)),
        out_shape=jax.ShapeDtypeStruct((n, d), dtype),
        compiler_params=_params("parallel"),
        name="row_norm",
    )(x, g.reshape(1, d))


def _proj_kernel(a_ref, w_ref, *rest, q_scale, rotary):
    if rotary:
        c_ref, s1_ref, s2_ref, o_ref, wb_ref = rest
    else:
        o_ref, wb_ref = rest
    j = pl.program_id(0)

    @pl.when(pl.program_id(1) == 0)
    def _():
        wb_ref[...] = w_ref[...].astype(BF16)

    def emit(rope, scale):
        if rope:
            c = c_ref[...]
            s1 = s1_ref[...]
            s2 = s2_ref[...]
        for cols in _chunks(wb_ref.shape[1]):
            acc = jnp.dot(a_ref[...], wb_ref[:, cols], preferred_element_type=F32)
            for h in range((cols.stop - cols.start) // LANES):
                t = acc[:, h * LANES:(h + 1) * LANES]
                if rope:
                    lo = pltpu.roll(t, LANES - ROT_DIM // 2, 1)
                    hi = pltpu.roll(t, ROT_DIM // 2, 1)
                    t = (t * c + lo * s1 + hi * s2) * scale
                o_ref[cols.start // LANES + h] = t.astype(o_ref.dtype)

    if rotary:
        pl.when(j == 0)(lambda: emit(True, q_scale))
        pl.when(j == 1)(lambda: emit(True, 1.0))
        pl.when(j >= 2)(lambda: emit(False, 1.0))
    else:
        emit(False, 1.0)


def _project(h, w_in, layer, tabs, width_b):
    n, d = h.shape
    p = w_in.shape[2]
    tm = _pick(n, (512, 256, 128))
    q_scale = math.log2(math.e) * HEAD_DIM ** -0.5

    def call(tn, col0, n_tiles, rotary, dtype, name):
        assert col0 % tn == 0
        hpt = tn // LANES
        in_specs = [pl.BlockSpec((tm, d), lambda j, i: (i, 0)),
                    pl.BlockSpec((None, d, tn), lambda j, i: (layer, 0, col0 // tn + j))]
        args = [h, w_in]
        if rotary:
            in_specs += [pl.BlockSpec((tm, LANES), lambda j, i: (i, 0))] * 3
            args += list(tabs)
        return pl.pallas_call(
            functools.partial(_proj_kernel, q_scale=q_scale, rotary=rotary),
            grid=(n_tiles, n // tm),
            in_specs=in_specs,
            out_specs=pl.BlockSpec((hpt, tm, LANES), lambda j, i: (j, i, 0)),
            out_shape=jax.ShapeDtypeStruct((n_tiles * hpt, n, LANES), dtype),
            scratch_shapes=[pltpu.VMEM((d, tn), BF16)],
            compiler_params=_params("arbitrary", "arbitrary"),
            name=name,
        )(*args)

    qkv = call(width_b, 0, 3, True, F32, "in_proj_qkv")
    tn_uv = _pick(math.gcd(3 * width_b, p - 3 * width_b), (512, 256, 128))
    uv = call(tn_uv, 3 * width_b, (p - 3 * width_b) // tn_uv, False, BF16, "in_proj_uv")
    return qkv, uv


def _attn_kernel(q_ref, k_ref, v_ref, g_ref, o_ref, qd_ref, kd_ref, vd_ref, bias_ref, acc_ref, m_ref, l_ref, *,
                 seq, tq, unroll):
    win = tq + 2 * RADIUS
    copy_rows = 256
    qq = lax.broadcasted_iota(jnp.int32, (tq, win), 0)
    kk = lax.broadcasted_iota(jnp.int32, (tq, win), 1)
    for case in range(3):
        bias_ref[case] = jnp.where(jnp.abs(kk - qq - case * RADIUS) <= RADIUS, 0.0, NEG)
    ones = jnp.ones((win, LANES), BF16)

    for d in DILATIONS:
        sub_len = seq // d
        nt = sub_len // tq
        first = d == DILATIONS[0]
        last = d == DILATIONS[-1]

        def deinterleave(c, carry, d=d, sub_len=sub_len):
            per = sub_len // copy_rows
            r = c // per
            m0 = (c % per) * copy_rows
            src = pl.ds(r + d * m0, copy_rows, stride=d) if d > 1 else pl.ds(pl.multiple_of(m0, copy_rows), copy_rows)
            dst = pl.ds(pl.multiple_of(c * copy_rows, copy_rows), copy_rows)
            qd_ref[dst, :] = q_ref[src, :].astype(BF16)
            kd_ref[dst, :] = k_ref[src, :].astype(BF16)
            vd_ref[dst, :] = v_ref[src, :].astype(BF16)
            return carry

        per = copy_rows // d
        ii = lax.broadcasted_iota(jnp.int32, (copy_rows, copy_rows), 0)
        jj = lax.broadcasted_iota(jnp.int32, (copy_rows, copy_rows), 1)
        perm = (jj == (ii % per) * d + ii // per).astype(BF16)

        def deinterleave_mxu(bi, carry, d=d, sub_len=sub_len, per=per, perm=perm):
            rows = pl.ds(pl.multiple_of(bi * copy_rows, copy_rows), copy_rows)
            x = jnp.concatenate([ref[rows, :].astype(BF16) for ref in (q_ref, k_ref, v_ref)], axis=1)
            y = jnp.dot(perm, x, preferred_element_type=F32).astype(BF16)
            for a, dst_ref in enumerate((qd_ref, kd_ref, vd_ref)):
                for r in range(d):
                    dst = pl.ds(pl.multiple_of(r * sub_len + bi * per, per), per)
                    dst_ref[dst, :] = y[r * per:(r + 1) * per, a * LANES:(a + 1) * LANES]
            return carry

        if d >= 16 and copy_rows % (16 * d) == 0:
            lax.fori_loop(0, seq // copy_rows, deinterleave_mxu, 0, unroll=4)
        else:
            lax.fori_loop(0, seq // copy_rows, deinterleave, 0)

        def tile(t, d=d, sub_len=sub_len, nt=nt, first=first):
            r = t // nt
            ti = t % nt
            q0 = ti * tq
            k0 = jnp.clip(q0 - RADIUS, 0, sub_len - win)
            case = jnp.where(ti == 0, 0, jnp.where(ti == nt - 1, 2, 1))
            kbase = pl.multiple_of(r * sub_len + k0, RADIUS)
            if d == 1:
                rows = pl.ds(pl.multiple_of(q0, tq), tq)
            else:
                rows = pl.ds(r + d * q0, tq, stride=d)
            q = qd_ref[pl.ds(pl.multiple_of(t * tq, tq), tq), :]
            k = kd_ref[pl.ds(kbase, win), :]
            v = vd_ref[pl.ds(kbase, win), :]
            s = lax.dot_general(q, k, (((1,), (1,)), ((), ())), preferred_element_type=F32) + bias_ref[case]
            m_t = jnp.broadcast_to(jnp.max(s, axis=-1, keepdims=True), (tq, LANES))
            m_prev = None if first else m_ref[rows, :]
            m_new = m_t if first else jnp.maximum(m_prev, m_t)
            p = jnp.exp2(s - jnp.concatenate([m_new] * (win // LANES), axis=1)).astype(BF16)
            pv = jnp.dot(p, jnp.concatenate([v, ones], axis=1), preferred_element_type=F32)
            return rows, m_prev, m_new, pv[:, LANES:], pv[:, :LANES]

        def merge(rows, m_prev, m_new, l_t, acc_t, first=first, last=last):
            if first:
                l_new, acc_new = l_t, acc_t
            else:
                a = jnp.exp2(m_prev - m_new)
                l_new = a * l_ref[rows, :] + l_t
                acc_new = a * acc_ref[rows, :] + acc_t
            if last:
                o_ref[rows, :] = _rms(acc_new / l_new, g_ref[...]).astype(o_ref.dtype)
            else:
                m_ref[rows, :] = m_new
                l_ref[rows, :] = l_new
                acc_ref[rows, :] = acc_new

        def body(it, carry):
            parts = [tile(it * unroll + u) for u in range(unroll)]
            for part in parts:
                merge(*part)
            return carry

        lax.fori_loop(0, seq // (tq * unroll), body, 0)


def _attention(proj, g_out, batch, seq, n_heads_a, n_heads_b):
    tq = 128
    unroll = 16
    win = tq + 2 * RADIUS
    assert seq % (max(DILATIONS) * tq) == 0 and (seq // tq) % unroll == 0 and seq // max(DILATIONS) >= max(win, 256)
    n = batch * seq
    kern = functools.partial(_attn_kernel, seq=seq, tq=tq, unroll=unroll)
    blk = lambda off: pl.BlockSpec((None, seq, LANES), lambda b, h: (off + h, b, 0))
    return pl.pallas_call(
        kern,
        grid=(batch, n_heads_b),
        in_specs=[blk(0), blk(n_heads_b), blk(2 * n_heads_b),
                  pl.BlockSpec((None, 1, LANES), lambda b, h: (n_heads_a + h, 0, 0))],
        out_specs=pl.BlockSpec((seq, LANES), lambda b, h: (b, h)),
        out_shape=jax.ShapeDtypeStruct((n, n_heads_b * LANES), BF16),
        scratch_shapes=[pltpu.VMEM((seq, LANES), BF16)] * 3
        + [pltpu.VMEM((3, tq, win), F32)]
        + [pltpu.VMEM((seq, LANES), F32)] * 3,
        compiler_params=_params("parallel", "parallel"),
        name="dilated_attention",
    )(proj, proj, proj, g_out.reshape(-1, 1, LANES))


def _gmlp_kernel(u_ref, v_ref, ws_ref, bs_ref, gv_ref, go_ref, o_ref, *, n_heads_a, n_chunks):
    for h in range(n_heads_a):
        w = ws_ref[h].astype(BF16)
        b = bs_ref[h]
        gv = gv_ref[h]
        go = go_ref[h]
        for c in range(n_chunks):
            rows = slice(c * CHUNK, (c + 1) * CHUNK)
            u = _gelu(u_ref[h, rows, :].astype(F32))
            v = _rms(_gelu(v_ref[h, rows, :].astype(F32)), gv)
            sv = jnp.dot(w, v.astype(BF16), preferred_element_type=F32) + b
            o_ref[rows, h * LANES:(h + 1) * LANES] = _rms(u * sv, go).astype(o_ref.dtype)


def _gmlp(uv, w_s, b_s, g_v, g_out, n_heads_a):
    n = uv.shape[1]
    tt = _pick(n, (512, 256, 128))
    kern = functools.partial(_gmlp_kernel, n_heads_a=n_heads_a, n_chunks=tt // CHUNK)
    full = lambda shp: pl.BlockSpec(shp, lambda i: (0,) * len(shp))
    return pl.pallas_call(
        kern,
        grid=(n // tt,),
        in_specs=[pl.BlockSpec((n_heads_a, tt, LANES), lambda i: (0, i, 0)),
                  pl.BlockSpec((n_heads_a, tt, LANES), lambda i: (1, i, 0)),
                  full((n_heads_a, CHUNK, CHUNK)),
                  full((n_heads_a, CHUNK, 1)),
                  full((n_heads_a, 1, LANES)),
                  full((n_heads_a, 1, LANES))],
        out_specs=pl.BlockSpec((tt, n_heads_a * LANES), lambda i: (i, 0)),
        out_shape=jax.ShapeDtypeStruct((n, n_heads_a * LANES), BF16),
        compiler_params=_params("parallel"),
        name="gmlp_gating",
    )(uv, uv, w_s, b_s.reshape(n_heads_a, CHUNK, 1), g_v.reshape(n_heads_a, 1, LANES),
      g_out[:n_heads_a].reshape(n_heads_a, 1, LANES))


def _route(hn, r_ref, n_experts):
    r = r_ref[...]
    h_hi = hn.astype(BF16)
    h_lo = (hn - h_hi.astype(F32)).astype(BF16)
    r_hi = r.astype(BF16)
    r_lo = (r - r_hi.astype(F32)).astype(BF16)
    logits = (jnp.dot(h_hi, r_hi, preferred_element_type=F32) + jnp.dot(h_lo, r_hi, preferred_element_type=F32)
              + jnp.dot(h_hi, r_lo, preferred_element_type=F32))
    lane = lax.broadcasted_iota(jnp.int32, logits.shape, 1)
    lg = jnp.where(lane < n_experts, logits, -jnp.inf)
    v1 = jnp.max(lg, axis=-1, keepdims=True)
    i1 = jnp.min(jnp.where(lg == v1, lane, LANES), axis=-1, keepdims=True)
    lg2 = jnp.where(lane == i1, -jnp.inf, lg)
    v2 = jnp.max(lg2, axis=-1, keepdims=True)
    i2 = jnp.min(jnp.where(lg2 == v2, lane, LANES), axis=-1, keepdims=True)
    e2 = jnp.exp(v2 - v1)
    g1 = 1.0 / (1.0 + e2)
    g2 = e2 / (1.0 + e2)
    return jnp.where(lane == 0, i1.astype(F32),
                     jnp.where(lane == 1, i2.astype(F32),
                               jnp.where(lane == 2, g1, jnp.where(lane == 3, g2, 0.0))))


def _store_packed(hn, h_ref):
    tm, d = hn.shape
    wpt = d // (2 * LANES)
    bits = pltpu.bitcast(hn.astype(BF16).astype(F32), U32)
    for c in range(wpt):
        lo = bits[:, c * LANES:(c + 1) * LANES] >> 16
        hi = bits[:, (c + wpt) * LANES:(c + wpt + 1) * LANES] & jnp.uint32(0xFFFF0000)
        h_ref[pl.ds(c, tm, stride=wpt), :] = lo | hi


def _finish(xn, g_ref, r_ref, xo_ref, h_ref, route_ref, n_experts):
    xo_ref[...] = xn
    hn = _rms(xn, g_ref[...])
    if route_ref is None:
        h_ref[...] = hn.astype(h_ref.dtype)
    else:
        _store_packed(hn, h_ref)
        route_ref[...] = _route(hn, r_ref, n_experts)


def _out_kernel(*refs, n_experts):
    if n_experts:
        ya_ref, yb_ref, w_ref, x_ref, g_ref, r_ref, xo_ref, h_ref, route_ref, wb_ref = refs
    else:
        ya_ref, yb_ref, w_ref, x_ref, g_ref, xo_ref, h_ref, wb_ref = refs
        r_ref = route_ref = None

    @pl.when(pl.program_id(0) == 0)
    def _():
        wb_ref[...] = w_ref[...].astype(BF16)

    wa = ya_ref.shape[1]
    acc = jnp.dot(ya_ref[...], wb_ref[:wa, :], preferred_element_type=F32)
    acc += jnp.dot(yb_ref[...], wb_ref[wa:, :], preferred_element_type=F32)
    _finish(x_ref[...] + acc, g_ref, r_ref, xo_ref, h_ref, route_ref, n_experts)


def _pad_router(router):
    d, e = router.shape
    return jnp.zeros((d, LANES), F32).at[:, :e].set(router)


def _out_proj(ya, yb, w_out, layer, x, g, router):
    n, d = x.shape
    tm = _pick(n, (256, 128))
    row = lambda w: pl.BlockSpec((tm, w), lambda i: (i, 0))
    in_specs = [row(ya.shape[1]), row(yb.shape[1]),
                pl.BlockSpec((None, d, d), lambda i: (layer, 0, 0), pipeline_mode=pl.Buffered(1)),
                row(d), pl.BlockSpec((1, d), lambda i: (0, 0))]
    args = [ya, yb, w_out, x, g.reshape(1, d)]
    n_experts = 0
    if router is None:
        out_specs = [row(d), row(d)]
        out_shape = [jax.ShapeDtypeStruct((n, d), F32), jax.ShapeDtypeStruct((n, d), BF16)]
    else:
        n_experts = router.shape[1]
        wpt = d // (2 * LANES)
        in_specs.append(pl.BlockSpec((d, LANES), lambda i: (0, 0)))
        args.append(_pad_router(router))
        out_specs = [row(d), pl.BlockSpec((tm * wpt, LANES), lambda i: (i, 0)), row(LANES)]
        out_shape = [jax.ShapeDtypeStruct((n, d), F32), jax.ShapeDtypeStruct((n * wpt, LANES), U32),
                     jax.ShapeDtypeStruct((n, LANES), F32)]
    return pl.pallas_call(
        functools.partial(_out_kernel, n_experts=n_experts),
        grid=(n // tm,),
        in_specs=in_specs,
        out_specs=out_specs,
        out_shape=out_shape,
        scratch_shapes=[pltpu.VMEM((d, d), BF16)],
        compiler_params=_params("arbitrary"),
        name="out_proj",
    )(*args)


def _up_kernel(te_ref, first_ref, act_ref, a_ref, w1_ref, w3_ref, o_ref, w1b_ref, w3b_ref):
    i = pl.program_id(1)

    @pl.when(first_ref[i] == 1)
    def _():
        w1b_ref[...] = w1_ref[...].astype(BF16)
        w3b_ref[...] = w3_ref[...].astype(BF16)

    @pl.when(act_ref[i] == 1)
    def _():
        for cols in _chunks(o_ref.shape[1]):
            h1 = jnp.dot(a_ref[...], w1b_ref[:, cols], preferred_element_type=F32)
            h3 = jnp.dot(a_ref[...], w3b_ref[:, cols], preferred_element_type=F32)
            o_ref[:, cols] = (_silu(h1) * h3).astype(o_ref.dtype)

    @pl.when(act_ref[i] == 0)
    def _():
        o_ref[...] = jnp.zeros_like(o_ref)


def _up(a, w1, w3, layer, plan, tm):
    r, d = a.shape
    f = w1.shape[3]
    tn = _pick(f, (1024, 512, 256, 128))
    w_spec = pl.BlockSpec((None, None, d, tn), lambda j, i, te, first, act: (layer, te[i], 0, j))
    grid_spec = pltpu.PrefetchScalarGridSpec(
        num_scalar_prefetch=3,
        grid=(f // tn, r // tm),
        in_specs=[pl.BlockSpec((tm, d), lambda j, i, te, first, act: (i, 0)), w_spec, w_spec],
        out_specs=pl.BlockSpec((tm, tn), lambda j, i, te, first, act: (i, j)),
        scratch_shapes=[pltpu.VMEM((d, tn), BF16)] * 2,
    )
    return pl.pallas_call(
        _up_kernel,
        grid_spec=grid_spec,
        out_shape=jax.ShapeDtypeStruct((r, f), BF16),
        compiler_params=_params("arbitrary", "arbitrary"),
        name="swiglu_up",
    )(*plan, a, w1, w3)


def _down_kernel(te_ref, first_ref, act_ref, a_ref, w_ref, *rest, residual):
    if residual:
        x_ref, o_ref, wb_ref = rest
    else:
        o_ref, wb_ref = rest
    i = pl.program_id(1)

    @pl.when(first_ref[i] == 1)
    def _():
        wb_ref[...] = w_ref[...].astype(BF16)

    @pl.when(act_ref[i] == 1)
    def _():
        acc = jnp.dot(a_ref[...], wb_ref[...], preferred_element_type=F32)
        o_ref[...] = x_ref[...] + acc if residual else acc

    @pl.when(act_ref[i] == 0)
    def _():
        o_ref[...] = jnp.zeros_like(o_ref)


def _down(a, w2, layer, plan, tm, x=None):
    r, f = a.shape
    d = w2.shape[3]

    def vmem_bytes(tn, w_bufs):
        io_blocks = 2 * (tm * f * 2 + tm * tn * 4 * (2 if x is not None else 1))
        return f * tn * (4 * w_bufs + 2) + io_blocks

    tn, w_bufs = next((t, b) for t in (1024, 512, 256, 128) for b in (2, 1)
                      if d % t == 0 and vmem_bytes(t, b) <= DOWN_VMEM_BUDGET_BYTES)
    w_mode = {} if w_bufs == 2 else {"pipeline_mode": pl.Buffered(1)}
    blk = pl.BlockSpec((tm, tn), lambda j, i, te, first, act: (i, j))
    in_specs = [pl.BlockSpec((tm, f), lambda j, i, te, first, act: (i, 0)),
                pl.BlockSpec((None, None, f, tn), lambda j, i, te, first, act: (layer, te[i], 0, j), **w_mode)]
    args = [a, w2]
    if x is not None:
        in_specs.append(blk)
        args.append(x)
    grid_spec = pltpu.PrefetchScalarGridSpec(
        num_scalar_prefetch=3,
        grid=(d // tn, r // tm),
        in_specs=in_specs,
        out_specs=blk,
        scratch_shapes=[pltpu.VMEM((f, tn), BF16)],
    )
    return pl.pallas_call(
        functools.partial(_down_kernel, residual=x is not None),
        grid_spec=grid_spec,
        out_shape=jax.ShapeDtypeStruct((r, d), F32),
        compiler_params=_params("arbitrary", "arbitrary"),
        name="swiglu_down",
    )(*plan, *args)


def _single_group_plan(n_tiles):
    te = jnp.zeros((n_tiles,), jnp.int32)
    first = jnp.zeros((n_tiles,), jnp.int32).at[0].set(1)
    act = jnp.ones((n_tiles,), jnp.int32)
    return te, first, act


def _gather_kernel(src_ref, h_hbm, o_ref, buf_ref, sem, *, tg, wpt):
    i = pl.program_id(0)
    n_blocks = pl.num_programs(0)

    def row_copy(block, slot, r):
        tok = src_ref[block * tg + r]
        return pltpu.make_async_copy(h_hbm.at[pl.ds(tok * wpt, wpt)],
                                     buf_ref.at[pl.ds((slot * tg + r) * wpt, wpt)], sem.at[slot])

    def start_block(block, slot):
        def start(r, c):
            row_copy(block, slot, r).start()
            return c
        lax.fori_loop(0, tg, start, 0, unroll=8)

    @pl.when(i == 0)
    def _():
        start_block(0, 0)

    @pl.when(i + 1 < n_blocks)
    def _():
        start_block(i + 1, (i + 1) % 2)

    slot = i % 2

    def wait(r, c):
        row_copy(i, slot, r).wait()
        return c
    lax.fori_loop(0, tg, wait, 0, unroll=8)

    for c in range(wpt):
        w = buf_ref[pl.ds(slot * tg * wpt + c, tg, stride=wpt), :]
        o_ref[:, c * LANES:(c + 1) * LANES] = pltpu.bitcast(w << 16, F32).astype(o_ref.dtype)
        o_ref[:, (c + wpt) * LANES:(c + wpt + 1) * LANES] = (
            pltpu.bitcast(w & jnp.uint32(0xFFFF0000), F32).astype(o_ref.dtype))


def _gather_rows(hp, src, d):
    rp = src.shape[0]
    wpt = d // (2 * LANES)
    tg = _pick(rp, (256, 128))
    grid_spec = pltpu.PrefetchScalarGridSpec(
        num_scalar_prefetch=1,
        grid=(rp // tg,),
        in_specs=[pl.BlockSpec(memory_space=pl.ANY)],
        out_specs=pl.BlockSpec((tg, d), lambda i, src: (i, 0)),
        scratch_shapes=[pltpu.VMEM((2 * tg * wpt, LANES), U32), pltpu.SemaphoreType.DMA((2,))],
    )
    return pl.pallas_call(
        functools.partial(_gather_kernel, tg=tg, wpt=wpt),
        grid_spec=grid_spec,
        out_shape=jax.ShapeDtypeStruct((rp, d), BF16),
        compiler_params=_params("arbitrary"),
        name="moe_gather",
    )(src, hp)


def _combine_kernel(pos_ref, x_ref, route_ref, ys_hbm, g_ref, xo_ref, h_ref, buf_ref, sem, *, tt):
    base = pl.program_id(0) * tt

    def row_copy(r, k):
        return pltpu.make_async_copy(ys_hbm.at[pl.ds(pos_ref[TOP_K * (base + r) + k], 1)],
                                     buf_ref.at[k, pl.ds(r, 1)], sem.at[k])

    def start(r, c):
        for k in range(TOP_K):
            row_copy(r, k).start()
        return c

    def wait(r, c):
        for k in range(TOP_K):
            row_copy(r, k).wait()
        return c

    lax.fori_loop(0, tt, start, 0, unroll=4)
    lax.fori_loop(0, tt, wait, 0, unroll=4)
    route = route_ref[...]
    xn = x_ref[...] + route[:, 2:3] * buf_ref[0] + route[:, 3:4] * buf_ref[1]
    _finish(xn, g_ref, None, xo_ref, h_ref, None, 0)


def _combine(x, route, ys, pos, g, h_dtype):
    n, d = x.shape
    tt = _pick(n, (256, 128))
    row = pl.BlockSpec((tt, d), lambda i, pos: (i, 0))
    grid_spec = pltpu.PrefetchScalarGridSpec(
        num_scalar_prefetch=1,
        grid=(n // tt,),
        in_specs=[row, pl.BlockSpec((tt, LANES), lambda i, pos: (i, 0)),
                  pl.BlockSpec(memory_space=pl.ANY),
                  pl.BlockSpec((1, d), lambda i, pos: (0, 0))],
        out_specs=[row, row],
        scratch_shapes=[pltpu.VMEM((TOP_K, tt, d), F32), pltpu.SemaphoreType.DMA((TOP_K,))],
    )
    return pl.pallas_call(
        functools.partial(_combine_kernel, tt=tt),
        grid_spec=grid_spec,
        out_shape=[jax.ShapeDtypeStruct((n, d), F32), jax.ShapeDtypeStruct((n, d), h_dtype)],
        compiler_params=_params("arbitrary"),
        name="moe_combine",
    )(pos, x, route, ys, g.reshape(1, d))


def _tile_plan(gend, n_experts, rp, tm):
    start = jnp.arange(rp // tm, dtype=jnp.int32) * tm
    te = jnp.minimum(jnp.sum((start[:, None] >= gend[None, :]).astype(jnp.int32), axis=1), n_experts - 1)
    prev = jnp.concatenate([jnp.full((1,), -1, jnp.int32), te[:-1]])
    first = (te != prev).astype(jnp.int32)
    act = (start < gend[-1]).astype(jnp.int32)
    return te, first, act


def _dispatch_plan(route, n_experts, pad):
    n = route.shape[0]
    ef = route[:, :TOP_K].astype(jnp.int32).reshape(-1)
    onehot = (ef[:, None] == jnp.arange(n_experts, dtype=jnp.int32)[None, :]).astype(jnp.int32)
    csum = jnp.cumsum(onehot, axis=0)
    counts = csum[-1]
    rank = jnp.take_along_axis(csum, ef[:, None], axis=1)[:, 0] - 1
    padded = ((counts + pad - 1) // pad) * pad
    gend = jnp.cumsum(padded)
    gstart = gend - padded
    pos = (gstart[ef] + rank).astype(jnp.int32)
    rp = TOP_K * n + n_experts * pad
    src = jnp.zeros((rp,), jnp.int32).at[pos].set(jnp.arange(TOP_K * n, dtype=jnp.int32) // TOP_K)
    return pos, src, gend, rp


def kernel(x, positions, attn_norm, w_in, w_s, b_s, gmlp_norm, out_norm, w_out, ffn_norm, dense_w1, dense_w3,
           dense_w2, router, moe_w1, moe_w3, moe_w2, final_norm):
    batch, seq, d = x.shape
    depth = w_in.shape[0]
    n = batch * seq
    n_heads = d // HEAD_DIM
    n_heads_a = n_heads // 4
    n_heads_b = n_heads - n_heads_a
    width_b = n_heads_b * HEAD_DIM
    tm_up = _pick(n, (512, 256, 128))
    tm_down = _pick(n, (256, 128))

    xf = x.reshape(n, d)
    tabs = _rope_tables(positions)
    h = _norm(xf, attn_norm[0], BF16)
    for i in range(depth):
        moe = i % 2 == 1
        j = i // 2
        last = i == depth - 1
        next_g = final_norm if last else attn_norm[i + 1]
        next_dtype = F32 if last else BF16

        qkv, uv = _project(h, w_in, i, tabs, width_b)
        yb = _attention(qkv, out_norm[i], batch, seq, n_heads_a, n_heads_b)
        ya = _gmlp(uv, w_s[i], b_s[i], gmlp_norm[i], out_norm[i], n_heads_a)
        if not moe:
            xf, h2 = _out_proj(ya, yb, w_out, i, xf, ffn_norm[i], None)
            mid = _up(h2, dense_w1[:, None], dense_w3[:, None], j, _single_group_plan(n // tm_up), tm_up)
            xf = _down(mid, dense_w2[:, None], j, _single_group_plan(n // tm_down), tm_down, x=xf)
            h = _norm(xf, next_g, next_dtype)
        else:
            n_experts = router.shape[2]
            xf, h2p, route = _out_proj(ya, yb, w_out, i, xf, ffn_norm[i], router[j])
            pos, src, gend, rp = _dispatch_plan(route, n_experts, tm_up)
            xs = _gather_rows(h2p, src, d)
            mid = _up(xs, moe_w1, moe_w3, j, _tile_plan(gend, n_experts, rp, tm_up), tm_up)
            ys = _down(mid, moe_w2, j, _tile_plan(gend, n_experts, rp, tm_down), tm_down)
            xf, h = _combine(xf, route, ys, pos, next_g, next_dtype)
    return h.reshape(batch, seq, d)
```

```python
import functools
import math

import numpy as np
import jax
import jax.numpy as jnp
from jax import lax
from jax.experimental import pallas as pl
from jax.experimental.pallas import tpu as pltpu

HEAD_DIM = 128
CHUNK = 128
RADIUS = 64
DILATIONS = (16, 4, 1)
ROT_DIM = HEAD_DIM // 4
ROPE_THETA = 500000.0
TOP_K = 2
EPS = 1e-5
NEG = -1e30
LANES = 128
VMEM_LIMIT_BYTES = 56 * 1024 * 1024
DOWN_VMEM_BUDGET_BYTES = 48 * 1024 * 1024
COL_CHUNK = 256

F32 = jnp.float32
BF16 = jnp.bfloat16
U32 = jnp.uint32


def _pick(n, prefs):
    for p in prefs:
        if n % p == 0:
            return p
    raise ValueError(f"no tile in {prefs} divides {n}")


def _params(*sem):
    return pltpu.CompilerParams(dimension_semantics=sem, vmem_limit_bytes=VMEM_LIMIT_BYTES)


def _rms(x, g):
    ms = jnp.mean(x * x, axis=-1, keepdims=True)
    return x * lax.rsqrt(ms + EPS) * g


def _gelu(x):
    return 0.5 * x * (1.0 + jnp.tanh(math.sqrt(2.0 / math.pi) * (x + 0.044715 * (x * x * x))))


def _silu(x):
    hx = 0.5 * x
    return hx + hx * jnp.tanh(hx)


def _chunks(tn):
    cw = COL_CHUNK if tn % COL_CHUNK == 0 else LANES
    return [slice(c * cw, (c + 1) * cw) for c in range(tn // cw)]


def _rope_kernel(pos_ref, inv_ref, c_ref, s1_ref, s2_ref):
    ang = pos_ref[...].astype(F32) * inv_ref[...]
    lane = lax.broadcasted_iota(jnp.int32, ang.shape, 1)
    sn = jnp.sin(ang)
    c_ref[...] = jnp.cos(ang)
    s1_ref[...] = jnp.where(lane < ROT_DIM // 2, -sn, 0.0)
    s2_ref[...] = jnp.where(lane >= ROT_DIM // 2, sn, 0.0)


def _rope_tables(positions):
    n = positions.size
    half = ROT_DIM // 2
    inv = ROPE_THETA ** (-np.arange(0, ROT_DIM, 2, dtype=np.float32) / ROT_DIM)
    inv_lane = np.zeros((1, LANES), np.float32)
    inv_lane[0, :half] = inv
    inv_lane[0, half:ROT_DIM] = inv
    tm = _pick(n, (2048, 1024, 512, 256, 128))
    out = jax.ShapeDtypeStruct((n, LANES), F32)
    return pl.pallas_call(
        _rope_kernel,
        grid=(n // tm,),
        in_specs=[pl.BlockSpec((tm, 1), lambda i: (i, 0)),
                  pl.BlockSpec((1, LANES), lambda i: (0, 0))],
        out_specs=[pl.BlockSpec((tm, LANES), lambda i: (i, 0))] * 3,
        out_shape=[out, out, out],
        compiler_params=_params("parallel"),
        name="rope_tables",
    )(positions.reshape(n, 1), jnp.asarray(inv_lane))


def _norm_kernel(x_ref, g_ref, o_ref):
    o_ref[...] = _rms(x_ref[...], g_ref[...]).astype(o_ref.dtype)


def _norm(x, g, dtype):
    n, d = x.shape
    tm = _pick(n, (512, 256, 128))
    return pl.pallas_call(
        _norm_kernel,
        grid=(n // tm,),
        in_specs=[pl.BlockSpec((tm, d), lambda i: (i, 0)),
                  pl.BlockSpec((1, d), lambda i: (0, 0))],
        out_specs=pl.BlockSpec((tm, d), lambda i: (i, 0)),
        out_shape=jax.ShapeDtypeStruct((n, d), dtype),
        compiler_params=_params("parallel"),
        name="row_norm",
    )(x, g.reshape(1, d))


def _proj_kernel(a_ref, w_ref, *rest, q_scale, rotary):
    if rotary:
        c_ref, s1_ref, s2_ref, o_ref, wb_ref = rest
    else:
        o_ref, wb_ref = rest
    j = pl.program_id(0)

    @pl.when(pl.program_id(1) == 0)
    def _():
        wb_ref[...] = w_ref[...].astype(BF16)

    def emit(rope, scale):
        if rope:
            c = c_ref[...]
            s1 = s1_ref[...]
            s2 = s2_ref[...]
        for cols in _chunks(wb_ref.shape[1]):
            acc = jnp.dot(a_ref[...], wb_ref[:, cols], preferred_element_type=F32)
            for h in range((cols.stop - cols.start) // LANES):
                t = acc[:, h * LANES:(h + 1) * LANES]
                if rope:
                    lo = pltpu.roll(t, LANES - ROT_DIM // 2, 1)
                    hi = pltpu.roll(t, ROT_DIM // 2, 1)
                    t = (t * c + lo * s1 + hi * s2) * scale
                o_ref[cols.start // LANES + h] = t.astype(o_ref.dtype)

    if rotary:
        pl.when(j == 0)(lambda: emit(True, q_scale))
        pl.when(j == 1)(lambda: emit(True, 1.0))
        pl.when(j >= 2)(lambda: emit(False, 1.0))
    else:
        emit(False, 1.0)


def _project(h, w_in, layer, tabs, width_b):
    n, d = h.shape
    p = w_in.shape[2]
    tm = _pick(n, (1024, 512, 256, 128))
    q_scale = math.log2(math.e) * HEAD_DIM ** -0.5

    def call(tn, col0, n_tiles, rotary, dtype, name):
        assert col0 % tn == 0
        hpt = tn // LANES
        in_specs = [pl.BlockSpec((tm, d), lambda j, i: (i, 0)),
                    pl.BlockSpec((None, d, tn), lambda j, i: (layer, 0, col0 // tn + j))]
        args = [h, w_in]
        if rotary:
            in_specs += [pl.BlockSpec((tm, LANES), lambda j, i: (i, 0))] * 3
            args += list(tabs)
        return pl.pallas_call(
            functools.partial(_proj_kernel, q_scale=q_scale, rotary=rotary),
            grid=(n_tiles, n // tm),
            in_specs=in_specs,
            out_specs=pl.BlockSpec((hpt, tm, LANES), lambda j, i: (j, i, 0)),
            out_shape=jax.ShapeDtypeStruct((n_tiles * hpt, n, LANES), dtype),
            scratch_shapes=[pltpu.VMEM((d, tn), BF16)],
            compiler_params=_params("arbitrary", "arbitrary"),
            name=name,
        )(*args)

    qkv = call(width_b, 0, 3, True, BF16, "in_proj_qkv")
    tn_uv = _pick(math.gcd(3 * width_b, p - 3 * width_b), (512, 256, 128))
    uv = call(tn_uv, 3 * width_b, (p - 3 * width_b) // tn_uv, False, BF16, "in_proj_uv")
    return qkv, uv


def _attn_kernel(q_ref, k_ref, v_ref, g_ref, o_ref, qd_ref, kd_ref, vd_ref, bias_ref, acc_ref, m_ref, l_ref, *,
                 seq, tq, unroll):
    win = tq + 2 * RADIUS
    copy_rows = 256
    qq = lax.broadcasted_iota(jnp.int32, (tq, win), 0)
    kk = lax.broadcasted_iota(jnp.int32, (tq, win), 1)
    for case in range(3):
        bias_ref[case] = jnp.where(jnp.abs(kk - qq - case * RADIUS) <= RADIUS, 0.0, NEG)
    ones = jnp.ones((win, LANES), BF16)

    for d in DILATIONS:
        sub_len = seq // d
        nt = sub_len // tq
        first = d == DILATIONS[0]
        last = d == DILATIONS[-1]

        per = copy_rows // d
        ii = lax.broadcasted_iota(jnp.int32, (copy_rows, copy_rows), 0)
        jj = lax.broadcasted_iota(jnp.int32, (copy_rows, copy_rows), 1)
        perm = (jj == (ii % per) * d + ii // per).astype(BF16)

        def deinterleave(bi, carry, d=d, sub_len=sub_len, per=per, perm=perm):
            rows = pl.ds(pl.multiple_of(bi * copy_rows, copy_rows), copy_rows)
            x = jnp.concatenate([ref[rows, :] for ref in (q_ref, k_ref, v_ref)], axis=1)
            y = jnp.dot(perm, x, preferred_element_type=F32).astype(BF16)
            for a, dst_ref in enumerate((qd_ref, kd_ref, vd_ref)):
                for r in range(d):
                    dst = pl.ds(pl.multiple_of(r * sub_len + bi * per, per), per)
                    dst_ref[dst, :] = y[r * per:(r + 1) * per, a * LANES:(a + 1) * LANES]
            return carry

        if d > 1:
            lax.fori_loop(0, seq // copy_rows, deinterleave, 0, unroll=4)
        qs_ref, ks_ref, vs_ref = (q_ref, k_ref, v_ref) if d == 1 else (qd_ref, kd_ref, vd_ref)

        def tile(t, d=d, sub_len=sub_len, nt=nt, first=first, qs_ref=qs_ref, ks_ref=ks_ref, vs_ref=vs_ref):
            r = t // nt
            ti = t % nt
            q0 = ti * tq
            k0 = jnp.clip(q0 - RADIUS, 0, sub_len - win)
            case = jnp.where(ti == 0, 0, jnp.where(ti == nt - 1, 2, 1))
            kbase = pl.multiple_of(r * sub_len + k0, RADIUS)
            if d == 1:
                rows = pl.ds(pl.multiple_of(q0, tq), tq)
            else:
                rows = pl.ds(r + d * q0, tq, stride=d)
            q = qs_ref[pl.ds(pl.multiple_of(t * tq, tq), tq), :]
            k = ks_ref[pl.ds(kbase, win), :]
            v = vs_ref[pl.ds(kbase, win), :]
            s = lax.dot_general(q, k, (((1,), (1,)), ((), ())), preferred_element_type=F32) + bias_ref[case]
            m_t = jnp.broadcast_to(jnp.max(s, axis=-1, keepdims=True), (tq, LANES))
            m_prev = None if first else m_ref[rows, :]
            m_new = m_t if first else jnp.maximum(m_prev, m_t)
            p = jnp.exp2(s - jnp.concatenate([m_new] * (win // LANES), axis=1)).astype(BF16)
            pv = jnp.dot(p, jnp.concatenate([v, ones], axis=1), preferred_element_type=F32)
            return rows, m_prev, m_new, pv[:, LANES:], pv[:, :LANES]

        def merge(rows, m_prev, m_new, l_t, acc_t, first=first, last=last):
            if first:
                l_new, acc_new = l_t, acc_t
            else:
                a = jnp.exp2(m_prev - m_new)
                l_new = a * l_ref[rows, :] + l_t
                acc_new = a * acc_ref[rows, :] + acc_t
            if last:
                o_ref[rows, :] = _rms(acc_new / l_new, g_ref[...]).astype(o_ref.dtype)
            else:
                m_ref[rows, :] = m_new
                l_ref[rows, :] = l_new
                acc_ref[rows, :] = acc_new

        def body(it, carry):
            parts = [tile(it * unroll + u) for u in range(unroll)]
            for part in parts:
                merge(*part)
            return carry

        lax.fori_loop(0, seq // (tq * unroll), body, 0)


def _attention(proj, g_out, batch, seq, n_heads_a, n_heads_b):
    tq = 128
    unroll = 16
    win = tq + 2 * RADIUS
    assert seq % (max(DILATIONS) * tq) == 0 and (seq // tq) % unroll == 0 and seq // max(DILATIONS) >= max(win, 256)
    n = batch * seq
    kern = functools.partial(_attn_kernel, seq=seq, tq=tq, unroll=unroll)
    blk = lambda off: pl.BlockSpec((None, seq, LANES), lambda b, h: (off + h, b, 0))
    return pl.pallas_call(
        kern,
        grid=(batch, n_heads_b),
        in_specs=[blk(0), blk(n_heads_b), blk(2 * n_heads_b),
                  pl.BlockSpec((None, 1, LANES), lambda b, h: (n_heads_a + h, 0, 0))],
        out_specs=pl.BlockSpec((seq, LANES), lambda b, h: (b, h)),
        out_shape=jax.ShapeDtypeStruct((n, n_heads_b * LANES), BF16),
        scratch_shapes=[pltpu.VMEM((seq, LANES), BF16)] * 3
        + [pltpu.VMEM((3, tq, win), F32)]
        + [pltpu.VMEM((seq, LANES), F32)] * 3,
        compiler_params=_params("parallel", "parallel"),
        name="dilated_attention",
    )(proj, proj, proj, g_out.reshape(-1, 1, LANES))


def _gmlp_kernel(u_ref, v_ref, ws_ref, bs_ref, gv_ref, go_ref, o_ref, *, n_heads_a, n_chunks):
    for h in range(n_heads_a):
        w = ws_ref[h].astype(BF16)
        b = bs_ref[h]
        gv = gv_ref[h]
        go = go_ref[h]
        for c in range(n_chunks):
            rows = slice(c * CHUNK, (c + 1) * CHUNK)
            u = _gelu(u_ref[h, rows, :].astype(F32))
            v = _rms(_gelu(v_ref[h, rows, :].astype(F32)), gv)
            sv = jnp.dot(w, v.astype(BF16), preferred_element_type=F32) + b
            o_ref[rows, h * LANES:(h + 1) * LANES] = _rms(u * sv, go).astype(o_ref.dtype)


def _gmlp(uv, w_s, b_s, g_v, g_out, n_heads_a):
    n = uv.shape[1]
    tt = _pick(n, (512, 256, 128))
    kern = functools.partial(_gmlp_kernel, n_heads_a=n_heads_a, n_chunks=tt // CHUNK)
    full = lambda shp: pl.BlockSpec(shp, lambda i: (0,) * len(shp))
    return pl.pallas_call(
        kern,
        grid=(n // tt,),
        in_specs=[pl.BlockSpec((n_heads_a, tt, LANES), lambda i: (0, i, 0)),
                  pl.BlockSpec((n_heads_a, tt, LANES), lambda i: (1, i, 0)),
                  full((n_heads_a, CHUNK, CHUNK)),
                  full((n_heads_a, CHUNK, 1)),
                  full((n_heads_a, 1, LANES)),
                  full((n_heads_a, 1, LANES))],
        out_specs=pl.BlockSpec((tt, n_heads_a * LANES), lambda i: (i, 0)),
        out_shape=jax.ShapeDtypeStruct((n, n_heads_a * LANES), BF16),
        compiler_params=_params("parallel"),
        name="gmlp_gating",
    )(uv, uv, w_s, b_s.reshape(n_heads_a, CHUNK, 1), g_v.reshape(n_heads_a, 1, LANES),
      g_out[:n_heads_a].reshape(n_heads_a, 1, LANES))


def _route(hn, r_ref, n_experts):
    r = r_ref[...]
    h_hi = hn.astype(BF16)
    h_lo = (hn - h_hi.astype(F32)).astype(BF16)
    r_hi = r.astype(BF16)
    r_lo = (r - r_hi.astype(F32)).astype(BF16)
    logits = (jnp.dot(h_hi, r_hi, preferred_element_type=F32) + jnp.dot(h_lo, r_hi, preferred_element_type=F32)
              + jnp.dot(h_hi, r_lo, preferred_element_type=F32))
    lane = lax.broadcasted_iota(jnp.int32, logits.shape, 1)
    lg = jnp.where(lane < n_experts, logits, -jnp.inf)
    v1 = jnp.max(lg, axis=-1, keepdims=True)
    i1 = jnp.min(jnp.where(lg == v1, lane, LANES), axis=-1, keepdims=True)
    lg2 = jnp.where(lane == i1, -jnp.inf, lg)
    v2 = jnp.max(lg2, axis=-1, keepdims=True)
    i2 = jnp.min(jnp.where(lg2 == v2, lane, LANES), axis=-1, keepdims=True)
    e2 = jnp.exp(v2 - v1)
    g1 = 1.0 / (1.0 + e2)
    g2 = e2 / (1.0 + e2)
    return jnp.where(lane == 0, i1.astype(F32),
                     jnp.where(lane == 1, i2.astype(F32),
                               jnp.where(lane == 2, g1, jnp.where(lane == 3, g2, 0.0))))


def _store_packed(hn, h_ref):
    tm, d = hn.shape
    wpt = d // (2 * LANES)
    bits = pltpu.bitcast(hn.astype(BF16).astype(F32), U32)
    for c in range(wpt):
        lo = bits[:, c * LANES:(c + 1) * LANES] >> 16
        hi = bits[:, (c + wpt) * LANES:(c + wpt + 1) * LANES] & jnp.uint32(0xFFFF0000)
        h_ref[pl.ds(c, tm, stride=wpt), :] = lo | hi


def _finish(xn, g_ref, r_ref, xo_ref, h_ref, route_ref, n_experts):
    xo_ref[...] = xn
    hn = _rms(xn, g_ref[...])
    if route_ref is None:
        h_ref[...] = hn.astype(h_ref.dtype)
    else:
        _store_packed(hn, h_ref)
        route_ref[...] = _route(hn, r_ref, n_experts)


def _out_kernel(*refs, n_experts):
    if n_experts:
        ya_ref, yb_ref, w_ref, x_ref, g_ref, r_ref, xo_ref, h_ref, route_ref, wb_ref = refs
    else:
        ya_ref, yb_ref, w_ref, x_ref, g_ref, xo_ref, h_ref, wb_ref = refs
        r_ref = route_ref = None

    @pl.when(pl.program_id(0) == 0)
    def _():
        wb_ref[...] = w_ref[...].astype(BF16)

    wa = ya_ref.shape[1]
    acc = jnp.dot(ya_ref[...], wb_ref[:wa, :], preferred_element_type=F32)
    acc += jnp.dot(yb_ref[...], wb_ref[wa:, :], preferred_element_type=F32)
    _finish(x_ref[...] + acc, g_ref, r_ref, xo_ref, h_ref, route_ref, n_experts)


def _pad_router(router):
    d, e = router.shape
    return jnp.zeros((d, LANES), F32).at[:, :e].set(router)


def _out_proj(ya, yb, w_out, layer, x, g, router):
    n, d = x.shape
    tm = _pick(n, (256, 128))
    row = lambda w: pl.BlockSpec((tm, w), lambda i: (i, 0))
    in_specs = [row(ya.shape[1]), row(yb.shape[1]),
                pl.BlockSpec((None, d, d), lambda i: (layer, 0, 0), pipeline_mode=pl.Buffered(1)),
                row(d), pl.BlockSpec((1, d), lambda i: (0, 0))]
    args = [ya, yb, w_out, x, g.reshape(1, d)]
    n_experts = 0
    if router is None:
        out_specs = [row(d), row(d)]
        out_shape = [jax.ShapeDtypeStruct((n, d), F32), jax.ShapeDtypeStruct((n, d), BF16)]
    else:
        n_experts = router.shape[1]
        wpt = d // (2 * LANES)
        in_specs.append(pl.BlockSpec((d, LANES), lambda i: (0, 0)))
        args.append(_pad_router(router))
        out_specs = [row(d), pl.BlockSpec((tm * wpt, LANES), lambda i: (i, 0)), row(LANES)]
        out_shape = [jax.ShapeDtypeStruct((n, d), F32), jax.ShapeDtypeStruct((n * wpt, LANES), U32),
                     jax.ShapeDtypeStruct((n, LANES), F32)]
    return pl.pallas_call(
        functools.partial(_out_kernel, n_experts=n_experts),
        grid=(n // tm,),
        in_specs=in_specs,
        out_specs=out_specs,
        out_shape=out_shape,
        scratch_shapes=[pltpu.VMEM((d, d), BF16)],
        compiler_params=_params("arbitrary"),
        name="out_proj",
    )(*args)


def _up_kernel(te_ref, first_ref, act_ref, a_ref, w1_ref, w3_ref, o_ref, w1b_ref, w3b_ref):
    i = pl.program_id(1)

    @pl.when(first_ref[i] == 1)
    def _():
        w1b_ref[...] = w1_ref[...].astype(BF16)
        w3b_ref[...] = w3_ref[...].astype(BF16)

    @pl.when(act_ref[i] == 1)
    def _():
        for cols in _chunks(o_ref.shape[1]):
            h1 = jnp.dot(a_ref[...], w1b_ref[:, cols], preferred_element_type=F32)
            h3 = jnp.dot(a_ref[...], w3b_ref[:, cols], preferred_element_type=F32)
            o_ref[:, cols] = (_silu(h1) * h3).astype(o_ref.dtype)

    @pl.when(act_ref[i] == 0)
    def _():
        o_ref[...] = jnp.zeros_like(o_ref)


def _up(a, w1, w3, layer, plan, tm):
    r, d = a.shape
    f = w1.shape[3]
    tn = _pick(f, (1024, 512, 256, 128))
    w_spec = pl.BlockSpec((None, None, d, tn), lambda j, i, te, first, act: (layer, te[i], 0, j))
    grid_spec = pltpu.PrefetchScalarGridSpec(
        num_scalar_prefetch=3,
        grid=(f // tn, r // tm),
        in_specs=[pl.BlockSpec((tm, d), lambda j, i, te, first, act: (i, 0)), w_spec, w_spec],
        out_specs=pl.BlockSpec((tm, tn), lambda j, i, te, first, act: (i, j)),
        scratch_shapes=[pltpu.VMEM((d, tn), BF16)] * 2,
    )
    return pl.pallas_call(
        _up_kernel,
        grid_spec=grid_spec,
        out_shape=jax.ShapeDtypeStruct((r, f), BF16),
        compiler_params=_params("arbitrary", "arbitrary"),
        name="swiglu_up",
    )(*plan, a, w1, w3)


def _down_kernel(te_ref, first_ref, act_ref, a_ref, w_ref, *rest, residual):
    if residual:
        x_ref, o_ref, wb_ref = rest
    else:
        o_ref, wb_ref = rest
    i = pl.program_id(1)

    @pl.when(first_ref[i] == 1)
    def _():
        wb_ref[...] = w_ref[...].astype(BF16)

    @pl.when(act_ref[i] == 1)
    def _():
        acc = jnp.dot(a_ref[...], wb_ref[...], preferred_element_type=F32)
        o_ref[...] = x_ref[...] + acc if residual else acc

    @pl.when(act_ref[i] == 0)
    def _():
        o_ref[...] = jnp.zeros_like(o_ref)


def _down(a, w2, layer, plan, tm, x=None):
    r, f = a.shape
    d = w2.shape[3]

    def vmem_bytes(tn, w_bufs):
        io_blocks = 2 * (tm * f * 2 + tm * tn * 4 * (2 if x is not None else 1))
        return f * tn * (4 * w_bufs + 2) + io_blocks

    tn, w_bufs = next((t, b) for t in (1024, 512, 256, 128) for b in (2, 1)
                      if d % t == 0 and vmem_bytes(t, b) <= DOWN_VMEM_BUDGET_BYTES)
    w_mode = {} if w_bufs == 2 else {"pipeline_mode": pl.Buffered(1)}
    blk = pl.BlockSpec((tm, tn), lambda j, i, te, first, act: (i, j))
    in_specs = [pl.BlockSpec((tm, f), lambda j, i, te, first, act: (i, 0)),
                pl.BlockSpec((None, None, f, tn), lambda j, i, te, first, act: (layer, te[i], 0, j), **w_mode)]
    args = [a, w2]
    if x is not None:
        in_specs.append(blk)
        args.append(x)
    grid_spec = pltpu.PrefetchScalarGridSpec(
        num_scalar_prefetch=3,
        grid=(d // tn, r // tm),
        in_specs=in_specs,
        out_specs=blk,
        scratch_shapes=[pltpu.VMEM((f, tn), BF16)],
    )
    return pl.pallas_call(
        functools.partial(_down_kernel, residual=x is not None),
        grid_spec=grid_spec,
        out_shape=jax.ShapeDtypeStruct((r, d), F32),
        compiler_params=_params("arbitrary", "arbitrary"),
        name="swiglu_down",
    )(*plan, *args)


def _single_group_plan(n_tiles):
    te = jnp.zeros((n_tiles,), jnp.int32)
    first = jnp.zeros((n_tiles,), jnp.int32).at[0].set(1)
    act = jnp.ones((n_tiles,), jnp.int32)
    return te, first, act


def _gather_kernel(src_ref, h_hbm, o_ref, buf_ref, sem, *, tg, wpt):
    i = pl.program_id(0)
    n_blocks = pl.num_programs(0)

    def row_copy(block, slot, r):
        tok = src_ref[block * tg + r]
        return pltpu.make_async_copy(h_hbm.at[pl.ds(tok * wpt, wpt)],
                                     buf_ref.at[pl.ds((slot * tg + r) * wpt, wpt)], sem.at[slot])

    def start_block(block, slot):
        def start(r, c):
            row_copy(block, slot, r).start()
            return c
        lax.fori_loop(0, tg, start, 0, unroll=8)

    @pl.when(i == 0)
    def _():
        start_block(0, 0)

    @pl.when(i + 1 < n_blocks)
    def _():
        start_block(i + 1, (i + 1) % 2)

    slot = i % 2

    def wait(r, c):
        row_copy(i, slot, r).wait()
        return c
    lax.fori_loop(0, tg, wait, 0, unroll=8)

    for c in range(wpt):
        w = buf_ref[pl.ds(slot * tg * wpt + c, tg, stride=wpt), :]
        o_ref[:, c * LANES:(c + 1) * LANES] = pltpu.bitcast(w << 16, F32).astype(o_ref.dtype)
        o_ref[:, (c + wpt) * LANES:(c + wpt + 1) * LANES] = (
            pltpu.bitcast(w & jnp.uint32(0xFFFF0000), F32).astype(o_ref.dtype))


def _gather_rows(hp, src, d):
    rp = src.shape[0]
    wpt = d // (2 * LANES)
    tg = _pick(rp, (256, 128))
    grid_spec = pltpu.PrefetchScalarGridSpec(
        num_scalar_prefetch=1,
        grid=(rp // tg,),
        in_specs=[pl.BlockSpec(memory_space=pl.ANY)],
        out_specs=pl.BlockSpec((tg, d), lambda i, src: (i, 0)),
        scratch_shapes=[pltpu.VMEM((2 * tg * wpt, LANES), U32), pltpu.SemaphoreType.DMA((2,))],
    )
    return pl.pallas_call(
        functools.partial(_gather_kernel, tg=tg, wpt=wpt),
        grid_spec=grid_spec,
        out_shape=jax.ShapeDtypeStruct((rp, d), BF16),
        compiler_params=_params("arbitrary"),
        name="moe_gather",
    )(src, hp)


def _combine_kernel(pos_ref, x_ref, route_ref, ys_hbm, g_ref, xo_ref, h_ref, buf_ref, sem, *, tt):
    base = pl.program_id(0) * tt

    def row_copy(r, k):
        return pltpu.make_async_copy(ys_hbm.at[pl.ds(pos_ref[TOP_K * (base + r) + k], 1)],
                                     buf_ref.at[k, pl.ds(r, 1)], sem.at[k])

    def start(r, c):
        for k in range(TOP_K):
            row_copy(r, k).start()
        return c

    def wait(r, c):
        for k in range(TOP_K):
            row_copy(r, k).wait()
        return c

    lax.fori_loop(0, tt, start, 0, unroll=4)
    lax.fori_loop(0, tt, wait, 0, unroll=4)
    route = route_ref[...]
    xn = x_ref[...] + route[:, 2:3] * buf_ref[0] + route[:, 3:4] * buf_ref[1]
    _finish(xn, g_ref, None, xo_ref, h_ref, None, 0)


def _combine(x, route, ys, pos, g, h_dtype):
    n, d = x.shape
    tt = _pick(n, (256, 128))
    row = pl.BlockSpec((tt, d), lambda i, pos: (i, 0))
    grid_spec = pltpu.PrefetchScalarGridSpec(
        num_scalar_prefetch=1,
        grid=(n // tt,),
        in_specs=[row, pl.BlockSpec((tt, LANES), lambda i, pos: (i, 0)),
                  pl.BlockSpec(memory_space=pl.ANY),
                  pl.BlockSpec((1, d), lambda i, pos: (0, 0))],
        out_specs=[row, row],
        scratch_shapes=[pltpu.VMEM((TOP_K, tt, d), F32), pltpu.SemaphoreType.DMA((TOP_K,))],
    )
    return pl.pallas_call(
        functools.partial(_combine_kernel, tt=tt),
        grid_spec=grid_spec,
        out_shape=[jax.ShapeDtypeStruct((n, d), F32), jax.ShapeDtypeStruct((n, d), h_dtype)],
        compiler_params=_params("arbitrary"),
        name="moe_combine",
    )(pos, x, route, ys, g.reshape(1, d))


def _tile_plan(gend, n_experts, rp, tm):
    start = jnp.arange(rp // tm, dtype=jnp.int32) * tm
    te = jnp.minimum(jnp.sum((start[:, None] >= gend[None, :]).astype(jnp.int32), axis=1), n_experts - 1)
    prev = jnp.concatenate([jnp.full((1,), -1, jnp.int32), te[:-1]])
    first = (te != prev).astype(jnp.int32)
    act = (start < gend[-1]).astype(jnp.int32)
    return te, first, act


def _dispatch_plan(route, n_experts, pad):
    n = route.shape[0]
    ef = route[:, :TOP_K].astype(jnp.int32).reshape(-1)
    onehot = (ef[:, None] == jnp.arange(n_experts, dtype=jnp.int32)[None, :]).astype(jnp.int32)
    csum = jnp.cumsum(onehot, axis=0)
    counts = csum[-1]
    rank = jnp.take_along_axis(csum, ef[:, None], axis=1)[:, 0] - 1
    padded = ((counts + pad - 1) // pad) * pad
    gend = jnp.cumsum(padded)
    gstart = gend - padded
    pos = (gstart[ef] + rank).astype(jnp.int32)
    rp = TOP_K * n + n_experts * pad
    src = jnp.zeros((rp,), jnp.int32).at[pos].set(jnp.arange(TOP_K * n, dtype=jnp.int32) // TOP_K)
    return pos, src, gend, rp


def kernel(x, positions, attn_norm, w_in, w_s, b_s, gmlp_norm, out_norm, w_out, ffn_norm, dense_w1, dense_w3,
           dense_w2, router, moe_w1, moe_w3, moe_w2, final_norm):
    batch, seq, d = x.shape
    depth = w_in.shape[0]
    n = batch * seq
    n_heads = d // HEAD_DIM
    n_heads_a = n_heads // 4
    n_heads_b = n_heads - n_heads_a
    width_b = n_heads_b * HEAD_DIM
    tm_up = _pick(n, (512, 256, 128))
    tm_down = _pick(n, (256, 128))

    xf = x.reshape(n, d)
    tabs = _rope_tables(positions)
    h = _norm(xf, attn_norm[0], BF16)
    for i in range(depth):
        moe = i % 2 == 1
        j = i // 2
        last = i == depth - 1
        next_g = final_norm if last else attn_norm[i + 1]
        next_dtype = F32 if last else BF16

        qkv, uv = _project(h, w_in, i, tabs, width_b)
        yb = _attention(qkv, out_norm[i], batch, seq, n_heads_a, n_heads_b)
        ya = _gmlp(uv, w_s[i], b_s[i], gmlp_norm[i], out_norm[i], n_heads_a)
        if not moe:
            xf, h2 = _out_proj(ya, yb, w_out, i, xf, ffn_norm[i], None)
            tm_dense = _pick(n, (1024, 512, 256, 128))
            mid = _up(h2, dense_w1[:, None], dense_w3[:, None], j, _single_group_plan(n // tm_dense), tm_dense)
            xf = _down(mid, dense_w2[:, None], j, _single_group_plan(n // tm_down), tm_down, x=xf)
            h = _norm(xf, next_g, next_dtype)
        else:
            n_experts = router.shape[2]
            xf, h2p, route = _out_proj(ya, yb, w_out, i, xf, ffn_norm[i], router[j])
            pos, src, gend, rp = _dispatch_plan(route, n_experts, tm_up)
            xs = _gather_rows(h2p, src, d)
            mid = _up(xs, moe_w1, moe_w3, j, _tile_plan(gend, n_experts, rp, tm_up), tm_up)
            ys = _down(mid, moe_w2, j, _tile_plan(gend, n_experts, rp, tm_down), tm_down)
            xf, h = _combine(xf, route, ys, pos, next_g, next_dtype)
    return h.reshape(batch, seq, d)
```

```python
import functools
import math

import numpy as np
import jax
import jax.numpy as jnp
from jax import lax
from jax.experimental import pallas as pl
from jax.experimental.pallas import tpu as pltpu

HEAD_DIM = 128
CHUNK = 128
RADIUS = 64
DILATIONS = (16, 4, 1)
ROT_DIM = HEAD_DIM // 4
ROPE_THETA = 500000.0
TOP_K = 2
EPS = 1e-5
NEG = -1e30
LANES = 128
VMEM_LIMIT_BYTES = 56 * 1024 * 1024
DOWN_VMEM_BUDGET_BYTES = 48 * 1024 * 1024
COL_CHUNK = 256
DOWN_COL_CHUNK = 512

F32 = jnp.float32
BF16 = jnp.bfloat16
U32 = jnp.uint32


def _pick(n, prefs):
    for p in prefs:
        if n % p == 0:
            return p
    raise ValueError(f"no tile in {prefs} divides {n}")


def _params(*sem):
    return pltpu.CompilerParams(dimension_semantics=sem, vmem_limit_bytes=VMEM_LIMIT_BYTES)


def _rms(x, g):
    ms = jnp.mean(x * x, axis=-1, keepdims=True)
    return x * lax.rsqrt(ms + EPS) * g


def _gelu(x):
    return 0.5 * x * (1.0 + jnp.tanh(math.sqrt(2.0 / math.pi) * (x + 0.044715 * (x * x * x))))


def _silu(x):
    hx = 0.5 * x
    return hx + hx * jnp.tanh(hx)


def _chunks(tn, width=None):
    cw = width or COL_CHUNK
    cw = cw if tn % cw == 0 else LANES
    return [slice(c * cw, (c + 1) * cw) for c in range(tn // cw)]


def _rope_kernel(pos_ref, inv_ref, c_ref, s1_ref, s2_ref):
    ang = pos_ref[...].astype(F32) * inv_ref[...]
    lane = lax.broadcasted_iota(jnp.int32, ang.shape, 1)
    sn = jnp.sin(ang)
    c_ref[...] = jnp.cos(ang)
    s1_ref[...] = jnp.where(lane < ROT_DIM // 2, -sn, 0.0)
    s2_ref[...] = jnp.where(lane >= ROT_DIM // 2, sn, 0.0)


def _rope_tables(positions):
    n = positions.size
    half = ROT_DIM // 2
    inv = ROPE_THETA ** (-np.arange(0, ROT_DIM, 2, dtype=np.float32) / ROT_DIM)
    inv_lane = np.zeros((1, LANES), np.float32)
    inv_lane[0, :half] = inv
    inv_lane[0, half:ROT_DIM] = inv
    tm = _pick(n, (2048, 1024, 512, 256, 128))
    out = jax.ShapeDtypeStruct((n, LANES), F32)
    return pl.pallas_call(
        _rope_kernel,
        grid=(n // tm,),
        in_specs=[pl.BlockSpec((tm, 1), lambda i: (i, 0)),
                  pl.BlockSpec((1, LANES), lambda i: (0, 0))],
        out_specs=[pl.BlockSpec((tm, LANES), lambda i: (i, 0))] * 3,
        out_shape=[out, out, out],
        compiler_params=_params("parallel"),
        name="rope_tables",
    )(positions.reshape(n, 1), jnp.asarray(inv_lane))


def _norm_kernel(x_ref, g_ref, o_ref):
    o_ref[...] = _rms(x_ref[...], g_ref[...]).astype(o_ref.dtype)


def _norm(x, g, dtype):
    n, d = x.shape
    tm = _pick(n, (512, 256, 128))
    return pl.pallas_call(
        _norm_kernel,
        grid=(n // tm,),
        in_specs=[pl.BlockSpec((tm, d), lambda i: (i, 0)),
                  pl.BlockSpec((1, d), lambda i: (0, 0))],
        out_specs=pl.BlockSpec((tm, d), lambda i: (i, 0)),
        out_shape=jax.ShapeDtypeStruct((n, d), dtype),
        compiler_params=_params("parallel"),
        name="row_norm",
    )(x, g.reshape(1, d))


def _proj_kernel(a_ref, w_ref, *rest, q_scale, rotary):
    if rotary:
        c_ref, s1_ref, s2_ref, o_ref, wb_ref = rest
    else:
        o_ref, wb_ref = rest
    j = pl.program_id(0)

    @pl.when(pl.program_id(1) == 0)
    def _():
        wb_ref[...] = w_ref[...].astype(BF16)

    def emit(rope, scale):
        if rope:
            c = c_ref[...]
            s1 = s1_ref[...]
            s2 = s2_ref[...]
        for cols in _chunks(wb_ref.shape[1]):
            acc = jnp.dot(a_ref[...], wb_ref[:, cols], preferred_element_type=F32)
            for h in range((cols.stop - cols.start) // LANES):
                t = acc[:, h * LANES:(h + 1) * LANES]
                if rope:
                    lo = pltpu.roll(t, LANES - ROT_DIM // 2, 1)
                    hi = pltpu.roll(t, ROT_DIM // 2, 1)
                    t = (t * c + lo * s1 + hi * s2) * scale
                o_ref[cols.start // LANES + h] = t.astype(o_ref.dtype)

    if rotary:
        pl.when(j == 0)(lambda: emit(True, q_scale))
        pl.when(j == 1)(lambda: emit(True, 1.0))
        pl.when(j >= 2)(lambda: emit(False, 1.0))
    else:
        emit(False, 1.0)


def _project(h, w_in, layer, tabs, width_b):
    n, d = h.shape
    p = w_in.shape[2]
    tm = _pick(n, (1024, 512, 256, 128))
    q_scale = math.log2(math.e) * HEAD_DIM ** -0.5

    def call(tn, col0, n_tiles, rotary, dtype, name):
        assert col0 % tn == 0
        hpt = tn // LANES
        in_specs = [pl.BlockSpec((tm, d), lambda j, i: (i, 0)),
                    pl.BlockSpec((None, d, tn), lambda j, i: (layer, 0, col0 // tn + j))]
        args = [h, w_in]
        if rotary:
            in_specs += [pl.BlockSpec((tm, LANES), lambda j, i: (i, 0))] * 3
            args += list(tabs)
        return pl.pallas_call(
            functools.partial(_proj_kernel, q_scale=q_scale, rotary=rotary),
            grid=(n_tiles, n // tm),
            in_specs=in_specs,
            out_specs=pl.BlockSpec((hpt, tm, LANES), lambda j, i: (j, i, 0)),
            out_shape=jax.ShapeDtypeStruct((n_tiles * hpt, n, LANES), dtype),
            scratch_shapes=[pltpu.VMEM((d, tn), BF16)],
            compiler_params=_params("arbitrary", "arbitrary"),
            name=name,
        )(*args)

    return call(width_b, 0, 3, True, BF16, "in_proj_qkv")


def _attn_kernel(q_ref, k_ref, v_ref, g_ref, o_ref, qd_ref, kd_ref, vd_ref, bias_ref, acc_ref, m_ref, l_ref, *,
                 seq, tq, unroll):
    win = tq + 2 * RADIUS
    copy_rows = 256
    qq = lax.broadcasted_iota(jnp.int32, (tq, win), 0)
    kk = lax.broadcasted_iota(jnp.int32, (tq, win), 1)
    for case in range(3):
        bias_ref[case] = jnp.where(jnp.abs(kk - qq - case * RADIUS) <= RADIUS, 0.0, NEG)
    ones = jnp.ones((win, LANES), BF16)

    for d in DILATIONS:
        sub_len = seq // d
        nt = sub_len // tq
        first = d == DILATIONS[0]
        last = d == DILATIONS[-1]

        per = copy_rows // d
        ii = lax.broadcasted_iota(jnp.int32, (copy_rows, copy_rows), 0)
        jj = lax.broadcasted_iota(jnp.int32, (copy_rows, copy_rows), 1)
        perm = (jj == (ii % per) * d + ii // per).astype(BF16)

        def deinterleave(bi, carry, d=d, sub_len=sub_len, per=per, perm=perm):
            rows = pl.ds(pl.multiple_of(bi * copy_rows, copy_rows), copy_rows)
            x = jnp.concatenate([ref[rows, :] for ref in (q_ref, k_ref, v_ref)], axis=1)
            y = jnp.dot(perm, x, preferred_element_type=F32).astype(BF16)
            for a, dst_ref in enumerate((qd_ref, kd_ref, vd_ref)):
                for r in range(d):
                    dst = pl.ds(pl.multiple_of(r * sub_len + bi * per, per), per)
                    dst_ref[dst, :] = y[r * per:(r + 1) * per, a * LANES:(a + 1) * LANES]
            return carry

        if d > 1:
            lax.fori_loop(0, seq // copy_rows, deinterleave, 0, unroll=4)
        qs_ref, ks_ref, vs_ref = (q_ref, k_ref, v_ref) if d == 1 else (qd_ref, kd_ref, vd_ref)

        def tile(t, d=d, sub_len=sub_len, nt=nt, first=first, qs_ref=qs_ref, ks_ref=ks_ref, vs_ref=vs_ref):
            r = t // nt
            ti = t % nt
            q0 = ti * tq
            k0 = jnp.clip(q0 - RADIUS, 0, sub_len - win)
            case = jnp.where(ti == 0, 0, jnp.where(ti == nt - 1, 2, 1))
            kbase = pl.multiple_of(r * sub_len + k0, RADIUS)
            if d == 1:
                rows = pl.ds(pl.multiple_of(q0, tq), tq)
            else:
                rows = pl.ds(r + d * q0, tq, stride=d)
            q = qs_ref[pl.ds(pl.multiple_of(t * tq, tq), tq), :]
            k = ks_ref[pl.ds(kbase, win), :]
            v = vs_ref[pl.ds(kbase, win), :]
            s = lax.dot_general(q, k, (((1,), (1,)), ((), ())), preferred_element_type=F32) + bias_ref[case]
            m_t = jnp.broadcast_to(jnp.max(s, axis=-1, keepdims=True), (tq, LANES))
            m_prev = None if first else m_ref[rows, :]
            m_new = m_t if first else jnp.maximum(m_prev, m_t)
            p = jnp.exp2(s - jnp.concatenate([m_new] * (win // LANES), axis=1)).astype(BF16)
            pv = jnp.dot(p, jnp.concatenate([v, ones], axis=1), preferred_element_type=F32)
            return rows, m_prev, m_new, pv[:, LANES:], pv[:, :LANES]

        def merge(rows, m_prev, m_new, l_t, acc_t, first=first, last=last):
            if first:
                l_new, acc_new = l_t, acc_t
            else:
                a = jnp.exp2(m_prev - m_new)
                l_new = a * l_ref[rows, :] + l_t
                acc_new = a * acc_ref[rows, :] + acc_t
            if last:
                o_ref[rows, :] = _rms(acc_new / l_new, g_ref[...]).astype(o_ref.dtype)
            else:
                m_ref[rows, :] = m_new
                l_ref[rows, :] = l_new
                acc_ref[rows, :] = acc_new

        def body(it, carry):
            parts = [tile(it * unroll + u) for u in range(unroll)]
            for part in parts:
                merge(*part)
            return carry

        lax.fori_loop(0, seq // (tq * unroll), body, 0)


def _attention(proj, g_out, batch, seq, n_heads_a, n_heads_b):
    tq = 128
    unroll = 16
    win = tq + 2 * RADIUS
    assert seq % (max(DILATIONS) * tq) == 0 and (seq // tq) % unroll == 0 and seq // max(DILATIONS) >= max(win, 256)
    n = batch * seq
    kern = functools.partial(_attn_kernel, seq=seq, tq=tq, unroll=unroll)
    blk = lambda off: pl.BlockSpec((None, seq, LANES), lambda b, h: (off + h, b, 0))
    return pl.pallas_call(
        kern,
        grid=(batch, n_heads_b),
        in_specs=[blk(0), blk(n_heads_b), blk(2 * n_heads_b),
                  pl.BlockSpec((None, 1, LANES), lambda b, h: (n_heads_a + h, 0, 0))],
        out_specs=pl.BlockSpec((seq, LANES), lambda b, h: (b, h)),
        out_shape=jax.ShapeDtypeStruct((n, n_heads_b * LANES), BF16),
        scratch_shapes=[pltpu.VMEM((seq, LANES), BF16)] * 3
        + [pltpu.VMEM((3, tq, win), F32)]
        + [pltpu.VMEM((seq, LANES), F32)] * 3,
        compiler_params=_params("parallel", "parallel"),
        name="dilated_attention",
    )(proj, proj, proj, g_out.reshape(-1, 1, LANES))


def _gmlp_kernel(h_ref, wu_ref, wv_ref, ws_ref, bs_ref, gv_ref, go_ref, o_ref, wub_ref, wvb_ref, *, n_heads_a,
                 n_chunks):
    @pl.when(pl.program_id(0) == 0)
    def _():
        wub_ref[...] = wu_ref[...].astype(BF16)
        wvb_ref[...] = wv_ref[...].astype(BF16)

    u_all = jnp.dot(h_ref[...], wub_ref[...], preferred_element_type=F32)
    v_all = jnp.dot(h_ref[...], wvb_ref[...], preferred_element_type=F32)
    for h in range(n_heads_a):
        w = ws_ref[h].astype(BF16)
        b = bs_ref[h]
        gv = gv_ref[h]
        go = go_ref[h]
        cols = slice(h * LANES, (h + 1) * LANES)
        for c in range(n_chunks):
            rows = slice(c * CHUNK, (c + 1) * CHUNK)
            u = _gelu(u_all[rows, cols])
            v = _rms(_gelu(v_all[rows, cols]), gv)
            sv = jnp.dot(w, v.astype(BF16), preferred_element_type=F32) + b
            o_ref[rows, cols] = _rms(u * sv, go).astype(o_ref.dtype)


def _gmlp(h, w_in, layer, col0, w_s, b_s, g_v, g_out, n_heads_a):
    n, d = h.shape
    width_a = n_heads_a * LANES
    assert col0 % width_a == 0
    tt = _pick(n, (512, 256, 128))
    kern = functools.partial(_gmlp_kernel, n_heads_a=n_heads_a, n_chunks=tt // CHUNK)
    full = lambda shp: pl.BlockSpec(shp, lambda i: (0,) * len(shp))
    w_spec = lambda blk: pl.BlockSpec((None, d, width_a), lambda i: (layer, 0, col0 // width_a + blk),
                                      pipeline_mode=pl.Buffered(1))
    return pl.pallas_call(
        kern,
        grid=(n // tt,),
        in_specs=[pl.BlockSpec((tt, d), lambda i: (i, 0)),
                  w_spec(0), w_spec(1),
                  full((n_heads_a, CHUNK, CHUNK)),
                  full((n_heads_a, CHUNK, 1)),
                  full((n_heads_a, 1, LANES)),
                  full((n_heads_a, 1, LANES))],
        out_specs=pl.BlockSpec((tt, n_heads_a * LANES), lambda i: (i, 0)),
        out_shape=jax.ShapeDtypeStruct((n, n_heads_a * LANES), BF16),
        scratch_shapes=[pltpu.VMEM((d, width_a), BF16)] * 2,
        compiler_params=_params("arbitrary"),
        name="gmlp_gating",
    )(h, w_in, w_in, w_s, b_s.reshape(n_heads_a, CHUNK, 1), g_v.reshape(n_heads_a, 1, LANES),
      g_out[:n_heads_a].reshape(n_heads_a, 1, LANES))


def _route(hn, r_ref, n_experts):
    r = r_ref[...]
    h_hi = hn.astype(BF16)
    h_lo = (hn - h_hi.astype(F32)).astype(BF16)
    r_hi = r.astype(BF16)
    r_lo = (r - r_hi.astype(F32)).astype(BF16)
    logits = (jnp.dot(h_hi, r_hi, preferred_element_type=F32) + jnp.dot(h_lo, r_hi, preferred_element_type=F32)
              + jnp.dot(h_hi, r_lo, preferred_element_type=F32))
    lane = lax.broadcasted_iota(jnp.int32, logits.shape, 1)
    lg = jnp.where(lane < n_experts, logits, -jnp.inf)
    v1 = jnp.max(lg, axis=-1, keepdims=True)
    i1 = jnp.min(jnp.where(lg == v1, lane, LANES), axis=-1, keepdims=True)
    lg2 = jnp.where(lane == i1, -jnp.inf, lg)
    v2 = jnp.max(lg2, axis=-1, keepdims=True)
    i2 = jnp.min(jnp.where(lg2 == v2, lane, LANES), axis=-1, keepdims=True)
    e2 = jnp.exp(v2 - v1)
    g1 = 1.0 / (1.0 + e2)
    g2 = e2 / (1.0 + e2)
    return jnp.where(lane == 0, i1.astype(F32),
                     jnp.where(lane == 1, i2.astype(F32),
                               jnp.where(lane == 2, g1, jnp.where(lane == 3, g2, 0.0))))


def _store_packed(hn, h_ref):
    tm, d = hn.shape
    wpt = d // (2 * LANES)
    bits = pltpu.bitcast(hn.astype(BF16).astype(F32), U32)
    for c in range(wpt):
        lo = bits[:, c * LANES:(c + 1) * LANES] >> 16
        hi = bits[:, (c + wpt) * LANES:(c + wpt + 1) * LANES] & jnp.uint32(0xFFFF0000)
        h_ref[pl.ds(c, tm, stride=wpt), :] = lo | hi


def _finish(xn, g_ref, r_ref, xo_ref, h_ref, route_ref, n_experts):
    xo_ref[...] = xn
    hn = _rms(xn, g_ref[...])
    if route_ref is None:
        h_ref[...] = hn.astype(h_ref.dtype)
    else:
        _store_packed(hn, h_ref)
        route_ref[...] = _route(hn, r_ref, n_experts)


def _out_kernel(*refs, n_experts):
    if n_experts:
        ya_ref, yb_ref, w_ref, x_ref, g_ref, r_ref, xo_ref, h_ref, route_ref, wb_ref = refs
    else:
        ya_ref, yb_ref, w_ref, x_ref, g_ref, xo_ref, h_ref, wb_ref = refs
        r_ref = route_ref = None

    @pl.when(pl.program_id(0) == 0)
    def _():
        wb_ref[...] = w_ref[...].astype(BF16)

    wa = ya_ref.shape[1]
    acc = jnp.dot(ya_ref[...], wb_ref[:wa, :], preferred_element_type=F32)
    acc += jnp.dot(yb_ref[...], wb_ref[wa:, :], preferred_element_type=F32)
    _finish(x_ref[...] + acc, g_ref, r_ref, xo_ref, h_ref, route_ref, n_experts)


def _pad_router(router):
    d, e = router.shape
    return jnp.zeros((d, LANES), F32).at[:, :e].set(router)


def _out_proj(ya, yb, w_out, layer, x, g, router):
    n, d = x.shape
    tm = _pick(n, (256, 128))
    row = lambda w: pl.BlockSpec((tm, w), lambda i: (i, 0))
    in_specs = [row(ya.shape[1]), row(yb.shape[1]),
                pl.BlockSpec((None, d, d), lambda i: (layer, 0, 0), pipeline_mode=pl.Buffered(1)),
                row(d), pl.BlockSpec((1, d), lambda i: (0, 0))]
    args = [ya, yb, w_out, x, g.reshape(1, d)]
    n_experts = 0
    if router is None:
        out_specs = [row(d), row(d)]
        out_shape = [jax.ShapeDtypeStruct((n, d), F32), jax.ShapeDtypeStruct((n, d), BF16)]
    else:
        n_experts = router.shape[1]
        wpt = d // (2 * LANES)
        in_specs.append(pl.BlockSpec((d, LANES), lambda i: (0, 0)))
        args.append(_pad_router(router))
        out_specs = [row(d), pl.BlockSpec((tm * wpt, LANES), lambda i: (i, 0)), row(LANES)]
        out_shape = [jax.ShapeDtypeStruct((n, d), F32), jax.ShapeDtypeStruct((n * wpt, LANES), U32),
                     jax.ShapeDtypeStruct((n, LANES), F32)]
    return pl.pallas_call(
        functools.partial(_out_kernel, n_experts=n_experts),
        grid=(n // tm,),
        in_specs=in_specs,
        out_specs=out_specs,
        out_shape=out_shape,
        scratch_shapes=[pltpu.VMEM((d, d), BF16)],
        compiler_params=_params("arbitrary"),
        name="out_proj",
    )(*args)


def _up_kernel(te_ref, first_ref, act_ref, a_ref, w1_ref, w3_ref, *rest, side_cast):
    if side_cast:
        w2_ref, o_ref, w2b_ref, w1b_ref, w3b_ref = rest
        w2b_ref[...] = w2_ref[...].astype(BF16)
    else:
        o_ref, w1b_ref, w3b_ref = rest
    i = pl.program_id(1)

    @pl.when(first_ref[i] == 1)
    def _():
        w1b_ref[...] = w1_ref[...].astype(BF16)
        w3b_ref[...] = w3_ref[...].astype(BF16)

    @pl.when(act_ref[i] == 1)
    def _():
        for cols in _chunks(o_ref.shape[1]):
            h1 = jnp.dot(a_ref[...], w1b_ref[:, cols], preferred_element_type=F32)
            h3 = jnp.dot(a_ref[...], w3b_ref[:, cols], preferred_element_type=F32)
            o_ref[:, cols] = (_silu(h1) * h3).astype(o_ref.dtype)

    @pl.when(act_ref[i] == 0)
    def _():
        o_ref[...] = jnp.zeros_like(o_ref)


def _up(a, w1, w3, layer, plan, tm, w2=None):
    r, d = a.shape
    f = w1.shape[3]
    tn = _pick(f, (1024, 512, 256, 128))
    n_tiles = r // tm
    w_spec = pl.BlockSpec((None, None, d, tn), lambda j, i, te, first, act: (layer, te[i], 0, j))
    in_specs = [pl.BlockSpec((tm, d), lambda j, i, te, first, act: (i, 0)), w_spec, w_spec]
    out_specs = [pl.BlockSpec((tm, tn), lambda j, i, te, first, act: (i, j))]
    out_shape = [jax.ShapeDtypeStruct((r, f), BF16)]
    args = [a, w1, w3]
    if w2 is not None:
        n_layers, n_experts, _, d_out = w2.shape
        rows = n_experts * f
        steps = (f // tn) * n_tiles
        slab = next(sl for sl in (128, 256, 512, 1024, 2048) if rows % sl == 0 and rows // sl <= steps)
        n_slabs = rows // slab
        slab_idx = lambda j, i, te, first, act: jnp.minimum(j * n_tiles + i, n_slabs - 1)
        in_specs.append(pl.BlockSpec((None, slab, d_out), lambda *g: (layer, slab_idx(*g), 0)))
        out_specs.append(pl.BlockSpec((slab, d_out), lambda *g: (slab_idx(*g), 0)))
        out_shape.append(jax.ShapeDtypeStruct((rows, d_out), BF16))
        args.append(w2.reshape(n_layers, rows, d_out))
    grid_spec = pltpu.PrefetchScalarGridSpec(
        num_scalar_prefetch=3,
        grid=(f // tn, n_tiles),
        in_specs=in_specs,
        out_specs=out_specs,
        scratch_shapes=[pltpu.VMEM((d, tn), BF16)] * 2,
    )
    outs = pl.pallas_call(
        functools.partial(_up_kernel, side_cast=w2 is not None),
        grid_spec=grid_spec,
        out_shape=out_shape,
        compiler_params=_params("arbitrary", "arbitrary"),
        name="swiglu_up",
    )(*plan, *args)
    if w2 is None:
        return outs[0]
    return outs[0], outs[1].reshape(n_experts, f, d_out)


def _down_kernel(te_ref, first_ref, act_ref, a_ref, w_ref, *rest, residual, cast):
    rest = list(rest)
    x_ref = rest.pop(0) if residual else None
    o_ref = rest.pop(0)
    i = pl.program_id(1)
    if cast:
        wb_ref = rest.pop(0)

        @pl.when(first_ref[i] == 1)
        def _():
            wb_ref[...] = w_ref[...].astype(BF16)
    else:
        wb_ref = w_ref

    @pl.when(act_ref[i] == 1)
    def _():
        for cols in _chunks(o_ref.shape[1], DOWN_COL_CHUNK):
            acc = jnp.dot(a_ref[...], wb_ref[:, cols], preferred_element_type=F32)
            o_ref[:, cols] = x_ref[:, cols] + acc if residual else acc

    @pl.when(act_ref[i] == 0)
    def _():
        o_ref[...] = jnp.zeros_like(o_ref)


def _down(a, w2, layer, plan, tm, x=None):
    r, f = a.shape
    d = w2.shape[3]
    cast = w2.dtype != BF16

    def vmem_bytes(tn, w_bufs):
        io_blocks = 2 * (tm * f * 2 + tm * tn * 4 * (2 if x is not None else 1))
        w_bytes = f * tn * (4 * w_bufs + 2) if cast else f * tn * 2 * w_bufs
        return w_bytes + io_blocks

    tn, w_bufs = next((t, b) for t in (1024, 512, 256, 128) for b in (2, 1)
                      if d % t == 0 and vmem_bytes(t, b) <= DOWN_VMEM_BUDGET_BYTES)
    w_mode = {} if w_bufs == 2 else {"pipeline_mode": pl.Buffered(1)}
    blk = pl.BlockSpec((tm, tn), lambda j, i, te, first, act: (i, j))
    in_specs = [pl.BlockSpec((tm, f), lambda j, i, te, first, act: (i, 0)),
                pl.BlockSpec((None, None, f, tn), lambda j, i, te, first, act: (layer, te[i], 0, j), **w_mode)]
    args = [a, w2]
    if x is not None:
        in_specs.append(blk)
        args.append(x)
    grid_spec = pltpu.PrefetchScalarGridSpec(
        num_scalar_prefetch=3,
        grid=(d // tn, r // tm),
        in_specs=in_specs,
        out_specs=blk,
        scratch_shapes=[pltpu.VMEM((f, tn), BF16)] if cast else [],
    )
    return pl.pallas_call(
        functools.partial(_down_kernel, residual=x is not None, cast=cast),
        grid_spec=grid_spec,
        out_shape=jax.ShapeDtypeStruct((r, d), F32),
        compiler_params=_params("arbitrary", "arbitrary"),
        name="swiglu_down",
    )(*plan, *args)


def _single_group_plan(n_tiles):
    te = jnp.zeros((n_tiles,), jnp.int32)
    first = jnp.zeros((n_tiles,), jnp.int32).at[0].set(1)
    act = jnp.ones((n_tiles,), jnp.int32)
    return te, first, act


def _gather_kernel(src_ref, h_hbm, o_ref, buf_ref, sem, *, tg, wpt):
    i = pl.program_id(0)
    n_blocks = pl.num_programs(0)

    def row_copy(block, slot, r):
        tok = src_ref[block * tg + r]
        return pltpu.make_async_copy(h_hbm.at[pl.ds(tok * wpt, wpt)],
                                     buf_ref.at[pl.ds((slot * tg + r) * wpt, wpt)], sem.at[slot])

    def start_block(block, slot):
        def start(r, c):
            row_copy(block, slot, r).start()
            return c
        lax.fori_loop(0, tg, start, 0, unroll=8)

    @pl.when(i == 0)
    def _():
        start_block(0, 0)

    @pl.when(i + 1 < n_blocks)
    def _():
        start_block(i + 1, (i + 1) % 2)

    slot = i % 2

    def wait(r, c):
        row_copy(i, slot, r).wait()
        return c
    lax.fori_loop(0, tg, wait, 0, unroll=8)

    for c in range(wpt):
        w = buf_ref[pl.ds(slot * tg * wpt + c, tg, stride=wpt), :]
        o_ref[:, c * LANES:(c + 1) * LANES] = pltpu.bitcast(w << 16, F32).astype(o_ref.dtype)
        o_ref[:, (c + wpt) * LANES:(c + wpt + 1) * LANES] = (
            pltpu.bitcast(w & jnp.uint32(0xFFFF0000), F32).astype(o_ref.dtype))


def _gather_rows(hp, src, d):
    rp = src.shape[0]
    wpt = d // (2 * LANES)
    tg = _pick(rp, (256, 128))
    grid_spec = pltpu.PrefetchScalarGridSpec(
        num_scalar_prefetch=1,
        grid=(rp // tg,),
        in_specs=[pl.BlockSpec(memory_space=pl.ANY)],
        out_specs=pl.BlockSpec((tg, d), lambda i, src: (i, 0)),
        scratch_shapes=[pltpu.VMEM((2 * tg * wpt, LANES), U32), pltpu.SemaphoreType.DMA((2,))],
    )
    return pl.pallas_call(
        functools.partial(_gather_kernel, tg=tg, wpt=wpt),
        grid_spec=grid_spec,
        out_shape=jax.ShapeDtypeStruct((rp, d), BF16),
        compiler_params=_params("arbitrary"),
        name="moe_gather",
    )(src, hp)


def _combine_kernel(pos_ref, x_ref, route_ref, ys_hbm, g_ref, xo_ref, h_ref, buf_ref, sem, *, tt):
    base = pl.program_id(0) * tt

    def row_copy(r, k):
        return pltpu.make_async_copy(ys_hbm.at[pl.ds(pos_ref[TOP_K * (base + r) + k], 1)],
                                     buf_ref.at[k, pl.ds(r, 1)], sem.at[k])

    def start(r, c):
        for k in range(TOP_K):
            row_copy(r, k).start()
        return c

    def wait(r, c):
        for k in range(TOP_K):
            row_copy(r, k).wait()
        return c

    lax.fori_loop(0, tt, start, 0, unroll=4)
    lax.fori_loop(0, tt, wait, 0, unroll=4)
    route = route_ref[...]
    xn = x_ref[...] + route[:, 2:3] * buf_ref[0] + route[:, 3:4] * buf_ref[1]
    _finish(xn, g_ref, None, xo_ref, h_ref, None, 0)


def _combine(x, route, ys, pos, g, h_dtype):
    n, d = x.shape
    tt = _pick(n, (256, 128))
    row = pl.BlockSpec((tt, d), lambda i, pos: (i, 0))
    grid_spec = pltpu.PrefetchScalarGridSpec(
        num_scalar_prefetch=1,
        grid=(n // tt,),
        in_specs=[row, pl.BlockSpec((tt, LANES), lambda i, pos: (i, 0)),
                  pl.BlockSpec(memory_space=pl.ANY),
                  pl.BlockSpec((1, d), lambda i, pos: (0, 0))],
        out_specs=[row, row],
        scratch_shapes=[pltpu.VMEM((TOP_K, tt, d), F32), pltpu.SemaphoreType.DMA((TOP_K,))],
    )
    return pl.pallas_call(
        functools.partial(_combine_kernel, tt=tt),
        grid_spec=grid_spec,
        out_shape=[jax.ShapeDtypeStruct((n, d), F32), jax.ShapeDtypeStruct((n, d), h_dtype)],
        compiler_params=_params("arbitrary"),
        name="moe_combine",
    )(pos, x, route, ys, g.reshape(1, d))


def _tile_plan(gend, n_experts, rp, tm):
    start = jnp.arange(rp // tm, dtype=jnp.int32) * tm
    te = jnp.minimum(jnp.sum((start[:, None] >= gend[None, :]).astype(jnp.int32), axis=1), n_experts - 1)
    prev = jnp.concatenate([jnp.full((1,), -1, jnp.int32), te[:-1]])
    first = (te != prev).astype(jnp.int32)
    act = (start < gend[-1]).astype(jnp.int32)
    return te, first, act


def _dispatch_plan(route, n_experts, pad):
    n = route.shape[0]
    ef = route[:, :TOP_K].astype(jnp.int32).reshape(-1)
    onehot = (ef[:, None] == jnp.arange(n_experts, dtype=jnp.int32)[None, :]).astype(jnp.int32)
    csum = jnp.cumsum(onehot, axis=0)
    counts = csum[-1]
    rank = jnp.take_along_axis(csum, ef[:, None], axis=1)[:, 0] - 1
    padded = ((counts + pad - 1) // pad) * pad
    gend = jnp.cumsum(padded)
    gstart = gend - padded
    pos = (gstart[ef] + rank).astype(jnp.int32)
    rp = TOP_K * n + n_experts * pad
    src = jnp.zeros((rp,), jnp.int32).at[pos].set(jnp.arange(TOP_K * n, dtype=jnp.int32) // TOP_K)
    return pos, src, gend, rp


def kernel(x, positions, attn_norm, w_in, w_s, b_s, gmlp_norm, out_norm, w_out, ffn_norm, dense_w1, dense_w3,
           dense_w2, router, moe_w1, moe_w3, moe_w2, final_norm):
    batch, seq, d = x.shape
    depth = w_in.shape[0]
    n = batch * seq
    n_heads = d // HEAD_DIM
    n_heads_a = n_heads // 4
    n_heads_b = n_heads - n_heads_a
    width_b = n_heads_b * HEAD_DIM
    tm_up = _pick(n, (512, 256, 128))
    tm_down = _pick(n, (256, 128))

    xf = x.reshape(n, d)
    tabs = _rope_tables(positions)
    h = _norm(xf, attn_norm[0], BF16)
    for i in range(depth):
        moe = i % 2 == 1
        j = i // 2
        last = i == depth - 1
        next_g = final_norm if last else attn_norm[i + 1]
        next_dtype = F32 if last else BF16

        qkv = _project(h, w_in, i, tabs, width_b)
        yb = _attention(qkv, out_norm[i], batch, seq, n_heads_a, n_heads_b)
        ya = _gmlp(h, w_in, i, 3 * width_b, w_s[i], b_s[i], gmlp_norm[i], out_norm[i], n_heads_a)
        if not moe:
            xf, h2 = _out_proj(ya, yb, w_out, i, xf, ffn_norm[i], None)
            tm_dense = _pick(n, (1024, 512, 256, 128))
            mid = _up(h2, dense_w1[:, None], dense_w3[:, None], j, _single_group_plan(n // tm_dense), tm_dense)
            xf = _down(mid, dense_w2[:, None], j, _single_group_plan(n // tm_down), tm_down, x=xf)
            h = _norm(xf, next_g, next_dtype)
        else:
            n_experts = router.shape[2]
            xf, h2p, route = _out_proj(ya, yb, w_out, i, xf, ffn_norm[i], router[j])
            pos, src, gend, rp = _dispatch_plan(route, n_experts, tm_up)
            xs = _gather_rows(h2p, src, d)
            mid, w2b = _up(xs, moe_w1, moe_w3, j, _tile_plan(gend, n_experts, rp, tm_up), tm_up, w2=moe_w2)
            ys = _down(mid, w2b[None], 0, _tile_plan(gend, n_experts, rp, tm_down), tm_down)
            xf, h = _combine(xf, route, ys, pos, next_g, next_dtype)
    return h.reshape(batch, seq, d)
```

```python
import functools
import math

import numpy as np
import jax
import jax.numpy as jnp
from jax import lax
from jax.experimental import pallas as pl
from jax.experimental.pallas import tpu as pltpu

HEAD_DIM = 128
CHUNK = 128
RADIUS = 64
DILATIONS = (16, 4, 1)
ROT_DIM = HEAD_DIM // 4
ROPE_THETA = 500000.0
TOP_K = 2
EPS = 1e-5
NEG = -1e30
LANES = 128
VMEM_LIMIT_BYTES = 56 * 1024 * 1024
DOWN_VMEM_BUDGET_BYTES = 48 * 1024 * 1024
COL_CHUNK = 256
DOWN_COL_CHUNK = 512

F32 = jnp.float32
BF16 = jnp.bfloat16
U32 = jnp.uint32


def _pick(n, prefs):
    for p in prefs:
        if n % p == 0:
            return p
    raise ValueError(f"no tile in {prefs} divides {n}")


def _params(*sem):
    return pltpu.CompilerParams(dimension_semantics=sem, vmem_limit_bytes=VMEM_LIMIT_BYTES)


def _rms(x, g):
    ms = jnp.mean(x * x, axis=-1, keepdims=True)
    return x * lax.rsqrt(ms + EPS) * g


def _gelu(x):
    return 0.5 * x * (1.0 + jnp.tanh(math.sqrt(2.0 / math.pi) * (x + 0.044715 * (x * x * x))))


def _silu(x):
    hx = 0.5 * x
    return hx + hx * jnp.tanh(hx)


def _chunks(tn, width=None):
    cw = width or COL_CHUNK
    cw = cw if tn % cw == 0 else LANES
    return [slice(c * cw, (c + 1) * cw) for c in range(tn // cw)]


def _rope_kernel(pos_ref, inv_ref, c_ref, s1_ref, s2_ref):
    ang = pos_ref[...].astype(F32) * inv_ref[...]
    lane = lax.broadcasted_iota(jnp.int32, ang.shape, 1)
    sn = jnp.sin(ang)
    c_ref[...] = jnp.cos(ang)
    s1_ref[...] = jnp.where(lane < ROT_DIM // 2, -sn, 0.0)
    s2_ref[...] = jnp.where(lane >= ROT_DIM // 2, sn, 0.0)


def _rope_tables(positions):
    n = positions.size
    half = ROT_DIM // 2
    inv = ROPE_THETA ** (-np.arange(0, ROT_DIM, 2, dtype=np.float32) / ROT_DIM)
    inv_lane = np.zeros((1, LANES), np.float32)
    inv_lane[0, :half] = inv
    inv_lane[0, half:ROT_DIM] = inv
    tm = _pick(n, (2048, 1024, 512, 256, 128))
    out = jax.ShapeDtypeStruct((n, LANES), F32)
    return pl.pallas_call(
        _rope_kernel,
        grid=(n // tm,),
        in_specs=[pl.BlockSpec((tm, 1), lambda i: (i, 0)),
                  pl.BlockSpec((1, LANES), lambda i: (0, 0))],
        out_specs=[pl.BlockSpec((tm, LANES), lambda i: (i, 0))] * 3,
        out_shape=[out, out, out],
        compiler_params=_params("parallel"),
        name="rope_tables",
    )(positions.reshape(n, 1), jnp.asarray(inv_lane))


def _norm_kernel(x_ref, g_ref, o_ref):
    o_ref[...] = _rms(x_ref[...], g_ref[...]).astype(o_ref.dtype)


def _norm(x, g, dtype):
    n, d = x.shape
    tm = _pick(n, (512, 256, 128))
    return pl.pallas_call(
        _norm_kernel,
        grid=(n // tm,),
        in_specs=[pl.BlockSpec((tm, d), lambda i: (i, 0)),
                  pl.BlockSpec((1, d), lambda i: (0, 0))],
        out_specs=pl.BlockSpec((tm, d), lambda i: (i, 0)),
        out_shape=jax.ShapeDtypeStruct((n, d), dtype),
        compiler_params=_params("parallel"),
        name="row_norm",
    )(x, g.reshape(1, d))


def _proj_kernel(a_ref, w_ref, *rest, q_scale, rotary):
    if rotary:
        c_ref, s1_ref, s2_ref, o_ref, wb_ref = rest
    else:
        o_ref, wb_ref = rest
    j = pl.program_id(0)

    @pl.when(pl.program_id(1) == 0)
    def _():
        wb_ref[...] = w_ref[...].astype(BF16)

    def emit(rope, scale):
        if rope:
            c = c_ref[...]
            s1 = s1_ref[...]
            s2 = s2_ref[...]
        for cols in _chunks(wb_ref.shape[1]):
            acc = jnp.dot(a_ref[...], wb_ref[:, cols], preferred_element_type=F32)
            for h in range((cols.stop - cols.start) // LANES):
                t = acc[:, h * LANES:(h + 1) * LANES]
                if rope:
                    lo = pltpu.roll(t, LANES - ROT_DIM // 2, 1)
                    hi = pltpu.roll(t, ROT_DIM // 2, 1)
                    t = (t * c + lo * s1 + hi * s2) * scale
                o_ref[cols.start // LANES + h] = t.astype(o_ref.dtype)

    if rotary:
        pl.when(j == 0)(lambda: emit(True, q_scale))
        pl.when(j == 1)(lambda: emit(True, 1.0))
        pl.when(j >= 2)(lambda: emit(False, 1.0))
    else:
        emit(False, 1.0)


def _project(h, w_in, layer, tabs, width_b):
    n, d = h.shape
    p = w_in.shape[2]
    tm = _pick(n, (1024, 512, 256, 128))
    q_scale = math.log2(math.e) * HEAD_DIM ** -0.5

    def call(tn, col0, n_tiles, rotary, dtype, name):
        assert col0 % tn == 0
        hpt = tn // LANES
        in_specs = [pl.BlockSpec((tm, d), lambda j, i: (i, 0)),
                    pl.BlockSpec((None, d, tn), lambda j, i: (layer, 0, col0 // tn + j))]
        args = [h, w_in]
        if rotary:
            in_specs += [pl.BlockSpec((tm, LANES), lambda j, i: (i, 0))] * 3
            args += list(tabs)
        return pl.pallas_call(
            functools.partial(_proj_kernel, q_scale=q_scale, rotary=rotary),
            grid=(n_tiles, n // tm),
            in_specs=in_specs,
            out_specs=pl.BlockSpec((hpt, tm, LANES), lambda j, i: (j, i, 0)),
            out_shape=jax.ShapeDtypeStruct((n_tiles * hpt, n, LANES), dtype),
            scratch_shapes=[pltpu.VMEM((d, tn), BF16)],
            compiler_params=_params("arbitrary", "arbitrary"),
            name=name,
        )(*args)

    return call(width_b, 0, 3, True, BF16, "in_proj_qkv")


def _attn_kernel(q_ref, k_ref, v_ref, g_ref, o_ref, qd_ref, kd_ref, vd_ref, bias_ref, acc_ref, m_ref, l_ref, *,
                 seq, tq, unroll):
    win = tq + 2 * RADIUS
    copy_rows = 256
    qq = lax.broadcasted_iota(jnp.int32, (tq, win), 0)
    kk = lax.broadcasted_iota(jnp.int32, (tq, win), 1)
    for case in range(3):
        bias_ref[case] = jnp.where(jnp.abs(kk - qq - case * RADIUS) <= RADIUS, 0.0, NEG)
    ones = jnp.ones((win, LANES), BF16)

    for d in DILATIONS:
        sub_len = seq // d
        nt = sub_len // tq
        first = d == DILATIONS[0]
        last = d == DILATIONS[-1]

        per = copy_rows // d
        ii = lax.broadcasted_iota(jnp.int32, (copy_rows, copy_rows), 0)
        jj = lax.broadcasted_iota(jnp.int32, (copy_rows, copy_rows), 1)
        perm = (jj == (ii % per) * d + ii // per).astype(BF16)

        def deinterleave(bi, carry, d=d, sub_len=sub_len, per=per, perm=perm):
            rows = pl.ds(pl.multiple_of(bi * copy_rows, copy_rows), copy_rows)
            x = jnp.concatenate([ref[rows, :] for ref in (q_ref, k_ref, v_ref)], axis=1)
            y = jnp.dot(perm, x, preferred_element_type=F32).astype(BF16)
            for a, dst_ref in enumerate((qd_ref, kd_ref, vd_ref)):
                for r in range(d):
                    dst = pl.ds(pl.multiple_of(r * sub_len + bi * per, per), per)
                    dst_ref[dst, :] = y[r * per:(r + 1) * per, a * LANES:(a + 1) * LANES]
            return carry

        if d > 1:
            lax.fori_loop(0, seq // copy_rows, deinterleave, 0, unroll=4)
        qs_ref, ks_ref, vs_ref = (q_ref, k_ref, v_ref) if d == 1 else (qd_ref, kd_ref, vd_ref)

        def tile(t, d=d, sub_len=sub_len, nt=nt, first=first, qs_ref=qs_ref, ks_ref=ks_ref, vs_ref=vs_ref):
            r = t // nt
            ti = t % nt
            q0 = ti * tq
            k0 = jnp.clip(q0 - RADIUS, 0, sub_len - win)
            case = jnp.where(ti == 0, 0, jnp.where(ti == nt - 1, 2, 1))
            kbase = pl.multiple_of(r * sub_len + k0, RADIUS)
            if d == 1:
                rows = pl.ds(pl.multiple_of(q0, tq), tq)
            else:
                rows = pl.ds(r + d * q0, tq, stride=d)
            q = qs_ref[pl.ds(pl.multiple_of(t * tq, tq), tq), :]
            k = ks_ref[pl.ds(kbase, win), :]
            v = vs_ref[pl.ds(kbase, win), :]
            s = lax.dot_general(q, k, (((1,), (1,)), ((), ())), preferred_element_type=F32) + bias_ref[case]
            m_t = jnp.broadcast_to(jnp.max(s, axis=-1, keepdims=True), (tq, LANES))
            m_prev = None if first else m_ref[rows, :]
            m_new = m_t if first else jnp.maximum(m_prev, m_t)
            p = jnp.exp2(s - jnp.concatenate([m_new] * (win // LANES), axis=1)).astype(BF16)
            pv = jnp.dot(p, jnp.concatenate([v, ones], axis=1), preferred_element_type=F32)
            return rows, m_prev, m_new, pv[:, LANES:], pv[:, :LANES]

        def merge(rows, m_prev, m_new, l_t, acc_t, first=first, last=last):
            if first:
                l_new, acc_new = l_t, acc_t
            else:
                a = jnp.exp2(m_prev - m_new)
                l_new = a * l_ref[rows, :] + l_t
                acc_new = a * acc_ref[rows, :] + acc_t
            if last:
                o_ref[rows, :] = _rms(acc_new / l_new, g_ref[...]).astype(o_ref.dtype)
            else:
                m_ref[rows, :] = m_new
                l_ref[rows, :] = l_new
                acc_ref[rows, :] = acc_new

        def body(it, carry):
            parts = [tile(it * unroll + u) for u in range(unroll)]
            for part in parts:
                merge(*part)
            return carry

        lax.fori_loop(0, seq // (tq * unroll), body, 0)


def _attention(proj, g_out, batch, seq, n_heads_a, n_heads_b):
    tq = 128
    unroll = 16
    win = tq + 2 * RADIUS
    assert seq % (max(DILATIONS) * tq) == 0 and (seq // tq) % unroll == 0 and seq // max(DILATIONS) >= max(win, 256)
    n = batch * seq
    kern = functools.partial(_attn_kernel, seq=seq, tq=tq, unroll=unroll)
    blk = lambda off: pl.BlockSpec((None, seq, LANES), lambda b, h: (off + h, b, 0))
    return pl.pallas_call(
        kern,
        grid=(batch, n_heads_b),
        in_specs=[blk(0), blk(n_heads_b), blk(2 * n_heads_b),
                  pl.BlockSpec((None, 1, LANES), lambda b, h: (n_heads_a + h, 0, 0))],
        out_specs=pl.BlockSpec((seq, LANES), lambda b, h: (b, h)),
        out_shape=jax.ShapeDtypeStruct((n, n_heads_b * LANES), BF16),
        scratch_shapes=[pltpu.VMEM((seq, LANES), BF16)] * 3
        + [pltpu.VMEM((3, tq, win), F32)]
        + [pltpu.VMEM((seq, LANES), F32)] * 3,
        compiler_params=_params("parallel", "parallel"),
        name="dilated_attention",
    )(proj, proj, proj, g_out.reshape(-1, 1, LANES))


def _gmlp_kernel(h_ref, wu_ref, wv_ref, ws_ref, bs_ref, gv_ref, go_ref, o_ref, wub_ref, wvb_ref, *, n_heads_a,
                 n_chunks):
    @pl.when(pl.program_id(0) == 0)
    def _():
        wub_ref[...] = wu_ref[...].astype(BF16)
        wvb_ref[...] = wv_ref[...].astype(BF16)

    u_all = jnp.dot(h_ref[...], wub_ref[...], preferred_element_type=F32)
    v_all = jnp.dot(h_ref[...], wvb_ref[...], preferred_element_type=F32)
    for h in range(n_heads_a):
        w = ws_ref[h].astype(BF16)
        b = bs_ref[h]
        gv = gv_ref[h]
        go = go_ref[h]
        cols = slice(h * LANES, (h + 1) * LANES)
        for c in range(n_chunks):
            rows = slice(c * CHUNK, (c + 1) * CHUNK)
            u = _gelu(u_all[rows, cols])
            v = _rms(_gelu(v_all[rows, cols]), gv)
            sv = jnp.dot(w, v.astype(BF16), preferred_element_type=F32) + b
            o_ref[rows, cols] = _rms(u * sv, go).astype(o_ref.dtype)


def _gmlp(h, w_in, layer, col0, w_s, b_s, g_v, g_out, n_heads_a):
    n, d = h.shape
    width_a = n_heads_a * LANES
    assert col0 % width_a == 0
    tt = _pick(n, (512, 256, 128))
    kern = functools.partial(_gmlp_kernel, n_heads_a=n_heads_a, n_chunks=tt // CHUNK)
    full = lambda shp: pl.BlockSpec(shp, lambda i: (0,) * len(shp))
    w_spec = lambda blk: pl.BlockSpec((None, d, width_a), lambda i: (layer, 0, col0 // width_a + blk),
                                      pipeline_mode=pl.Buffered(1))
    return pl.pallas_call(
        kern,
        grid=(n // tt,),
        in_specs=[pl.BlockSpec((tt, d), lambda i: (i, 0)),
                  w_spec(0), w_spec(1),
                  full((n_heads_a, CHUNK, CHUNK)),
                  full((n_heads_a, CHUNK, 1)),
                  full((n_heads_a, 1, LANES)),
                  full((n_heads_a, 1, LANES))],
        out_specs=pl.BlockSpec((tt, n_heads_a * LANES), lambda i: (i, 0)),
        out_shape=jax.ShapeDtypeStruct((n, n_heads_a * LANES), BF16),
        scratch_shapes=[pltpu.VMEM((d, width_a), BF16)] * 2,
        compiler_params=_params("arbitrary"),
        name="gmlp_gating",
    )(h, w_in, w_in, w_s, b_s.reshape(n_heads_a, CHUNK, 1), g_v.reshape(n_heads_a, 1, LANES),
      g_out[:n_heads_a].reshape(n_heads_a, 1, LANES))


def _route(hn, r_ref, n_experts):
    r = r_ref[...]
    h_hi = hn.astype(BF16)
    h_lo = (hn - h_hi.astype(F32)).astype(BF16)
    r_hi = r.astype(BF16)
    r_lo = (r - r_hi.astype(F32)).astype(BF16)
    logits = (jnp.dot(h_hi, r_hi, preferred_element_type=F32) + jnp.dot(h_lo, r_hi, preferred_element_type=F32)
              + jnp.dot(h_hi, r_lo, preferred_element_type=F32))
    lane = lax.broadcasted_iota(jnp.int32, logits.shape, 1)
    lg = jnp.where(lane < n_experts, logits, -jnp.inf)
    v1 = jnp.max(lg, axis=-1, keepdims=True)
    i1 = jnp.min(jnp.where(lg == v1, lane, LANES), axis=-1, keepdims=True)
    lg2 = jnp.where(lane == i1, -jnp.inf, lg)
    v2 = jnp.max(lg2, axis=-1, keepdims=True)
    i2 = jnp.min(jnp.where(lg2 == v2, lane, LANES), axis=-1, keepdims=True)
    e2 = jnp.exp(v2 - v1)
    g1 = 1.0 / (1.0 + e2)
    g2 = e2 / (1.0 + e2)
    return jnp.where(lane == 0, i1.astype(F32),
                     jnp.where(lane == 1, i2.astype(F32),
                               jnp.where(lane == 2, g1, jnp.where(lane == 3, g2, 0.0))))


def _store_packed(hn, h_ref):
    tm, d = hn.shape
    wpt = d // (2 * LANES)
    bits = pltpu.bitcast(hn.astype(BF16).astype(F32), U32)
    for c in range(wpt):
        lo = bits[:, c * LANES:(c + 1) * LANES] >> 16
        hi = bits[:, (c + wpt) * LANES:(c + wpt + 1) * LANES] & jnp.uint32(0xFFFF0000)
        h_ref[pl.ds(c, tm, stride=wpt), :] = lo | hi


def _finish(xn, g_ref, r_ref, xo_ref, h_ref, route_ref, n_experts):
    xo_ref[...] = xn
    hn = _rms(xn, g_ref[...])
    if route_ref is None:
        h_ref[...] = hn.astype(h_ref.dtype)
    else:
        _store_packed(hn, h_ref)
        route_ref[...] = _route(hn, r_ref, n_experts)


def _out_kernel(*refs, n_experts):
    if n_experts:
        ya_ref, yb_ref, w_ref, x_ref, g_ref, r_ref, xo_ref, h_ref, route_ref, wb_ref = refs
    else:
        ya_ref, yb_ref, w_ref, x_ref, g_ref, xo_ref, h_ref, wb_ref = refs
        r_ref = route_ref = None

    @pl.when(pl.program_id(0) == 0)
    def _():
        wb_ref[...] = w_ref[...].astype(BF16)

    wa = ya_ref.shape[1]
    acc = jnp.dot(ya_ref[...], wb_ref[:wa, :], preferred_element_type=F32)
    acc += jnp.dot(yb_ref[...], wb_ref[wa:, :], preferred_element_type=F32)
    _finish(x_ref[...] + acc, g_ref, r_ref, xo_ref, h_ref, route_ref, n_experts)


def _pad_router(router):
    d, e = router.shape
    return jnp.zeros((d, LANES), F32).at[:, :e].set(router)


def _out_proj(ya, yb, w_out, layer, x, g, router):
    n, d = x.shape
    tm = _pick(n, (256, 128))
    row = lambda w: pl.BlockSpec((tm, w), lambda i: (i, 0))
    in_specs = [row(ya.shape[1]), row(yb.shape[1]),
                pl.BlockSpec((None, d, d), lambda i: (layer, 0, 0), pipeline_mode=pl.Buffered(1)),
                row(d), pl.BlockSpec((1, d), lambda i: (0, 0))]
    args = [ya, yb, w_out, x, g.reshape(1, d)]
    n_experts = 0
    if router is None:
        out_specs = [row(d), row(d)]
        out_shape = [jax.ShapeDtypeStruct((n, d), F32), jax.ShapeDtypeStruct((n, d), BF16)]
    else:
        n_experts = router.shape[1]
        wpt = d // (2 * LANES)
        in_specs.append(pl.BlockSpec((d, LANES), lambda i: (0, 0)))
        args.append(_pad_router(router))
        out_specs = [row(d), pl.BlockSpec((tm * wpt, LANES), lambda i: (i, 0)), row(LANES)]
        out_shape = [jax.ShapeDtypeStruct((n, d), F32), jax.ShapeDtypeStruct((n * wpt, LANES), U32),
                     jax.ShapeDtypeStruct((n, LANES), F32)]
    return pl.pallas_call(
        functools.partial(_out_kernel, n_experts=n_experts),
        grid=(n // tm,),
        in_specs=in_specs,
        out_specs=out_specs,
        out_shape=out_shape,
        scratch_shapes=[pltpu.VMEM((d, d), BF16)],
        compiler_params=_params("arbitrary"),
        name="out_proj",
    )(*args)


def _up_kernel(te_ref, first_ref, act_ref, look_ref, a_ref, w1_ref, w3_ref, *rest, side_cast):
    if side_cast:
        w2_ref, o_ref, w2b_ref, w1b_ref, w3b_ref = rest
        w2b_ref[...] = w2_ref[...].astype(BF16)
    else:
        o_ref, w1b_ref, w3b_ref = rest
    i = pl.program_id(1)

    @pl.when(first_ref[i] == 1)
    def _():
        w1b_ref[...] = w1_ref[...].astype(BF16)
        w3b_ref[...] = w3_ref[...].astype(BF16)

    @pl.when(act_ref[i] == 1)
    def _():
        for cols in _chunks(o_ref.shape[1]):
            h1 = jnp.dot(a_ref[...], w1b_ref[:, cols], preferred_element_type=F32)
            h3 = jnp.dot(a_ref[...], w3b_ref[:, cols], preferred_element_type=F32)
            o_ref[:, cols] = (_silu(h1) * h3).astype(o_ref.dtype)

    @pl.when(act_ref[i] == 0)
    def _():
        o_ref[...] = jnp.zeros_like(o_ref)


def _up(a, w1, w3, layer, plan, tm, w2=None):
    r, d = a.shape
    f = w1.shape[3]
    tn = _pick(f, (1024, 512, 256, 128))
    n_tiles = r // tm
    te, first, act = plan
    nxt = jnp.concatenate([te[1:], te[-1:]])
    look = jnp.where(first == 0, nxt, te)
    plan = (te, first, act, look)
    w1_spec = pl.BlockSpec((None, None, d, tn), lambda j, i, te, first, act, look: (layer, look[i], 0, j))
    w3_spec = pl.BlockSpec((None, None, d, tn), lambda j, i, te, first, act, look: (layer, te[i], 0, j))
    in_specs = [pl.BlockSpec((tm, d), lambda j, i, *_: (i, 0)), w1_spec, w3_spec]
    out_specs = [pl.BlockSpec((tm, tn), lambda j, i, *_: (i, j))]
    out_shape = [jax.ShapeDtypeStruct((r, f), BF16)]
    args = [a, w1, w3]
    if w2 is not None:
        n_layers, n_experts, _, d_out = w2.shape
        rows = n_experts * f
        steps = (f // tn) * n_tiles
        slab = next(sl for sl in (128, 256, 512, 1024, 2048) if rows % sl == 0 and rows // sl <= steps)
        n_slabs = rows // slab
        slab_idx = lambda j, i, *_: jnp.minimum(j * n_tiles + i, n_slabs - 1)
        in_specs.append(pl.BlockSpec((None, slab, d_out), lambda *g: (layer, slab_idx(*g), 0)))
        out_specs.append(pl.BlockSpec((slab, d_out), lambda *g: (slab_idx(*g), 0)))
        out_shape.append(jax.ShapeDtypeStruct((rows, d_out), BF16))
        args.append(w2.reshape(n_layers, rows, d_out))
    grid_spec = pltpu.PrefetchScalarGridSpec(
        num_scalar_prefetch=4,
        grid=(f // tn, n_tiles),
        in_specs=in_specs,
        out_specs=out_specs,
        scratch_shapes=[pltpu.VMEM((d, tn), BF16)] * 2,
    )
    outs = pl.pallas_call(
        functools.partial(_up_kernel, side_cast=w2 is not None),
        grid_spec=grid_spec,
        out_shape=out_shape,
        compiler_params=_params("arbitrary", "arbitrary"),
        name="swiglu_up",
    )(*plan, *args)
    if w2 is None:
        return outs[0]
    return outs[0], outs[1].reshape(n_experts, f, d_out)


def _down_kernel(te_ref, first_ref, act_ref, a_ref, w_ref, *rest, residual, cast, norm):
    rest = list(rest)
    x_ref = rest.pop(0) if residual else None
    g_ref = rest.pop(0) if norm else None
    o_ref = rest.pop(0)
    h_ref = rest.pop(0) if norm else None
    i = pl.program_id(1)
    if cast:
        wb_ref = rest.pop(0)

        @pl.when(first_ref[i] == 1)
        def _():
            wb_ref[...] = w_ref[...].astype(BF16)
    else:
        wb_ref = w_ref

    @pl.when(act_ref[i] == 1)
    def _():
        for cols in _chunks(o_ref.shape[1], DOWN_COL_CHUNK):
            acc = jnp.dot(a_ref[...], wb_ref[:, cols], preferred_element_type=F32)
            o_ref[:, cols] = x_ref[:, cols] + acc if residual else acc
        if norm:
            h_ref[...] = _rms(o_ref[...], g_ref[...]).astype(h_ref.dtype)

    @pl.when(act_ref[i] == 0)
    def _():
        o_ref[...] = jnp.zeros_like(o_ref)
        if norm:
            h_ref[...] = jnp.zeros_like(h_ref)


def _down(a, w2, layer, plan, tm, x=None, norm=None):
    r, f = a.shape
    d = w2.shape[3]
    cast = w2.dtype != BF16

    def vmem_bytes(tn, w_bufs):
        out_blocks = 4 * (2 if x is not None else 1) + (4 if norm else 0)
        io_blocks = 2 * (tm * f * 2 + tm * tn * out_blocks)
        w_bytes = f * tn * (4 * w_bufs + 2) if cast else f * tn * 2 * w_bufs
        return w_bytes + io_blocks

    single_group = w2.shape[1] == 1
    order = [(d, 1)] if single_group else []
    order += [(t, b) for t in (1024, 512, 256, 128) for b in (2, 1)]
    tn, w_bufs = next((t, b) for t, b in order if d % t == 0 and vmem_bytes(t, b) <= DOWN_VMEM_BUDGET_BYTES)
    fuse_norm = norm is not None and tn == d
    w_mode = {} if w_bufs == 2 else {"pipeline_mode": pl.Buffered(1)}
    blk = pl.BlockSpec((tm, tn), lambda j, i, te, first, act: (i, j))
    in_specs = [pl.BlockSpec((tm, f), lambda j, i, te, first, act: (i, 0)),
                pl.BlockSpec((None, None, f, tn), lambda j, i, te, first, act: (layer, te[i], 0, j), **w_mode)]
    args = [a, w2]
    out_specs = [blk]
    out_shape = [jax.ShapeDtypeStruct((r, d), F32)]
    if x is not None:
        in_specs.append(blk)
        args.append(x)
    if fuse_norm:
        in_specs.append(pl.BlockSpec((1, d), lambda j, i, te, first, act: (0, 0)))
        args.append(norm[0].reshape(1, d))
        out_specs.append(blk)
        out_shape.append(jax.ShapeDtypeStruct((r, d), norm[1]))
    grid_spec = pltpu.PrefetchScalarGridSpec(
        num_scalar_prefetch=3,
        grid=(d // tn, r // tm),
        in_specs=in_specs,
        out_specs=out_specs,
        scratch_shapes=[pltpu.VMEM((f, tn), BF16)] if cast else [],
    )
    outs = pl.pallas_call(
        functools.partial(_down_kernel, residual=x is not None, cast=cast, norm=fuse_norm),
        grid_spec=grid_spec,
        out_shape=out_shape,
        compiler_params=_params("arbitrary", "arbitrary"),
        name="swiglu_down",
    )(*plan, *args)
    return outs[0], (outs[1] if fuse_norm else None)


def _single_group_plan(n_tiles):
    te = jnp.zeros((n_tiles,), jnp.int32)
    first = jnp.zeros((n_tiles,), jnp.int32).at[0].set(1)
    act = jnp.ones((n_tiles,), jnp.int32)
    return te, first, act


def _gather_kernel(src_ref, h_hbm, o_ref, buf_ref, sem, *, tg, wpt):
    i = pl.program_id(0)
    n_blocks = pl.num_programs(0)

    def row_copy(block, slot, r):
        tok = src_ref[block * tg + r]
        return pltpu.make_async_copy(h_hbm.at[pl.ds(tok * wpt, wpt)],
                                     buf_ref.at[pl.ds((slot * tg + r) * wpt, wpt)], sem.at[slot])

    def start_block(block, slot):
        def start(r, c):
            row_copy(block, slot, r).start()
            return c
        lax.fori_loop(0, tg, start, 0, unroll=8)

    @pl.when(i == 0)
    def _():
        start_block(0, 0)

    @pl.when(i + 1 < n_blocks)
    def _():
        start_block(i + 1, (i + 1) % 2)

    slot = i % 2

    def wait(r, c):
        row_copy(i, slot, r).wait()
        return c
    lax.fori_loop(0, tg, wait, 0, unroll=8)

    for c in range(wpt):
        w = buf_ref[pl.ds(slot * tg * wpt + c, tg, stride=wpt), :]
        o_ref[:, c * LANES:(c + 1) * LANES] = pltpu.bitcast(w << 16, F32).astype(o_ref.dtype)
        o_ref[:, (c + wpt) * LANES:(c + wpt + 1) * LANES] = (
            pltpu.bitcast(w & jnp.uint32(0xFFFF0000), F32).astype(o_ref.dtype))


def _gather_rows(hp, src, d):
    rp = src.shape[0]
    wpt = d // (2 * LANES)
    tg = _pick(rp, (256, 128))
    grid_spec = pltpu.PrefetchScalarGridSpec(
        num_scalar_prefetch=1,
        grid=(rp // tg,),
        in_specs=[pl.BlockSpec(memory_space=pl.ANY)],
        out_specs=pl.BlockSpec((tg, d), lambda i, src: (i, 0)),
        scratch_shapes=[pltpu.VMEM((2 * tg * wpt, LANES), U32), pltpu.SemaphoreType.DMA((2,))],
    )
    return pl.pallas_call(
        functools.partial(_gather_kernel, tg=tg, wpt=wpt),
        grid_spec=grid_spec,
        out_shape=jax.ShapeDtypeStruct((rp, d), BF16),
        compiler_params=_params("arbitrary"),
        name="moe_gather",
    )(src, hp)


def _combine_kernel(pos_ref, x_ref, route_ref, ys_hbm, g_ref, xo_ref, h_ref, buf_ref, sem, *, tt):
    base = pl.program_id(0) * tt

    def row_copy(r, k):
        return pltpu.make_async_copy(ys_hbm.at[pl.ds(pos_ref[TOP_K * (base + r) + k], 1)],
                                     buf_ref.at[k, pl.ds(r, 1)], sem.at[k])

    def start(r, c):
        for k in range(TOP_K):
            row_copy(r, k).start()
        return c

    def wait(r, c):
        for k in range(TOP_K):
            row_copy(r, k).wait()
        return c

    lax.fori_loop(0, tt, start, 0, unroll=4)
    lax.fori_loop(0, tt, wait, 0, unroll=4)
    route = route_ref[...]
    xn = x_ref[...] + route[:, 2:3] * buf_ref[0] + route[:, 3:4] * buf_ref[1]
    _finish(xn, g_ref, None, xo_ref, h_ref, None, 0)


def _combine(x, route, ys, pos, g, h_dtype):
    n, d = x.shape
    tt = _pick(n, (256, 128))
    row = pl.BlockSpec((tt, d), lambda i, pos: (i, 0))
    grid_spec = pltpu.PrefetchScalarGridSpec(
        num_scalar_prefetch=1,
        grid=(n // tt,),
        in_specs=[row, pl.BlockSpec((tt, LANES), lambda i, pos: (i, 0)),
                  pl.BlockSpec(memory_space=pl.ANY),
                  pl.BlockSpec((1, d), lambda i, pos: (0, 0))],
        out_specs=[row, row],
        scratch_shapes=[pltpu.VMEM((TOP_K, tt, d), F32), pltpu.SemaphoreType.DMA((TOP_K,))],
    )
    return pl.pallas_call(
        functools.partial(_combine_kernel, tt=tt),
        grid_spec=grid_spec,
        out_shape=[jax.ShapeDtypeStruct((n, d), F32), jax.ShapeDtypeStruct((n, d), h_dtype)],
        compiler_params=_params("arbitrary"),
        name="moe_combine",
    )(pos, x, route, ys, g.reshape(1, d))


def _tile_plan(gend, n_experts, rp, tm):
    start = jnp.arange(rp // tm, dtype=jnp.int32) * tm
    te = jnp.minimum(jnp.sum((start[:, None] >= gend[None, :]).astype(jnp.int32), axis=1), n_experts - 1)
    prev = jnp.concatenate([jnp.full((1,), -1, jnp.int32), te[:-1]])
    first = (te != prev).astype(jnp.int32)
    act = (start < gend[-1]).astype(jnp.int32)
    return te, first, act


def _dispatch_plan(route, n_experts, pad):
    n = route.shape[0]
    ef = route[:, :TOP_K].astype(jnp.int32).reshape(-1)
    onehot = (ef[:, None] == jnp.arange(n_experts, dtype=jnp.int32)[None, :]).astype(jnp.int32)
    csum = jnp.cumsum(onehot, axis=0)
    counts = csum[-1]
    rank = jnp.take_along_axis(csum, ef[:, None], axis=1)[:, 0] - 1
    padded = ((counts + pad - 1) // pad) * pad
    gend = jnp.cumsum(padded)
    gstart = gend - padded
    pos = (gstart[ef] + rank).astype(jnp.int32)
    rp = TOP_K * n + n_experts * pad
    src = jnp.zeros((rp,), jnp.int32).at[pos].set(jnp.arange(TOP_K * n, dtype=jnp.int32) // TOP_K)
    return pos, src, gend, rp


def kernel(x, positions, attn_norm, w_in, w_s, b_s, gmlp_norm, out_norm, w_out, ffn_norm, dense_w1, dense_w3,
           dense_w2, router, moe_w1, moe_w3, moe_w2, final_norm):
    batch, seq, d = x.shape
    depth = w_in.shape[0]
    n = batch * seq
    n_heads = d // HEAD_DIM
    n_heads_a = n_heads // 4
    n_heads_b = n_heads - n_heads_a
    width_b = n_heads_b * HEAD_DIM
    tm_up = _pick(n, (512, 256, 128))
    tm_down = _pick(n, (256, 128))

    xf = x.reshape(n, d)
    tabs = _rope_tables(positions)
    h = _norm(xf, attn_norm[0], BF16)
    for i in range(depth):
        moe = i % 2 == 1
        j = i // 2
        last = i == depth - 1
        next_g = final_norm if last else attn_norm[i + 1]
        next_dtype = F32 if last else BF16

        qkv = _project(h, w_in, i, tabs, width_b)
        yb = _attention(qkv, out_norm[i], batch, seq, n_heads_a, n_heads_b)
        ya = _gmlp(h, w_in, i, 3 * width_b, w_s[i], b_s[i], gmlp_norm[i], out_norm[i], n_heads_a)
        if not moe:
            xf, h2 = _out_proj(ya, yb, w_out, i, xf, ffn_norm[i], None)
            tm_dense = _pick(n, (1024, 512, 256, 128))
            mid, w2b = _up(h2, dense_w1[:, None], dense_w3[:, None], j, _single_group_plan(n // tm_dense), tm_dense,
                           w2=dense_w2[:, None])
            xf, h = _down(mid, w2b[None], 0, _single_group_plan(n // tm_down), tm_down, x=xf,
                          norm=(next_g, next_dtype))
            if h is None:
                h = _norm(xf, next_g, next_dtype)
        else:
            n_experts = router.shape[2]
            xf, h2p, route = _out_proj(ya, yb, w_out, i, xf, ffn_norm[i], router[j])
            pos, src, gend, rp = _dispatch_plan(route, n_experts, tm_up)
            xs = _gather_rows(h2p, src, d)
            mid, w2b = _up(xs, moe_w1, moe_w3, j, _tile_plan(gend, n_experts, rp, tm_up), tm_up, w2=moe_w2)
            ys, _ = _down(mid, w2b[None], 0, _tile_plan(gend, n_experts, rp, tm_down), tm_down)
            xf, h = _combine(xf, route, ys, pos, next_g, next_dtype)
    return h.reshape(batch, seq, d)
```

```python
import functools
import math

import numpy as np
import jax
import jax.numpy as jnp
from jax import lax
from jax.experimental import pallas as pl
from jax.experimental.pallas import tpu as pltpu

HEAD_DIM = 128
CHUNK = 128
RADIUS = 64
DILATIONS = (16, 4, 1)
ROT_DIM = HEAD_DIM // 4
ROPE_THETA = 500000.0
TOP_K = 2
EPS = 1e-5
NEG = -1e30
LANES = 128
VMEM_LIMIT_BYTES = 56 * 1024 * 1024
DOWN_VMEM_BUDGET_BYTES = 48 * 1024 * 1024
COL_CHUNK = 256
DOWN_COL_CHUNK = 512

F32 = jnp.float32
BF16 = jnp.bfloat16
U32 = jnp.uint32


def _pick(n, prefs):
    for p in prefs:
        if n % p == 0:
            return p
    raise ValueError(f"no tile in {prefs} divides {n}")


def _params(*sem):
    return pltpu.CompilerParams(dimension_semantics=sem, vmem_limit_bytes=VMEM_LIMIT_BYTES)


def _rms(x, g):
    ms = jnp.mean(x * x, axis=-1, keepdims=True)
    return x * lax.rsqrt(ms + EPS) * g


def _gelu(x):
    return 0.5 * x * (1.0 + jnp.tanh(math.sqrt(2.0 / math.pi) * (x + 0.044715 * (x * x * x))))


def _silu(x):
    hx = 0.5 * x
    return hx + hx * jnp.tanh(hx)


def _chunks(tn, width=None):
    cw = width or COL_CHUNK
    cw = cw if tn % cw == 0 else LANES
    return [slice(c * cw, (c + 1) * cw) for c in range(tn // cw)]


def _rope_kernel(pos_ref, inv_ref, c_ref, s1_ref, s2_ref):
    ang = pos_ref[...].astype(F32) * inv_ref[...]
    lane = lax.broadcasted_iota(jnp.int32, ang.shape, 1)
    sn = jnp.sin(ang)
    c_ref[...] = jnp.cos(ang)
    s1_ref[...] = jnp.where(lane < ROT_DIM // 2, -sn, 0.0)
    s2_ref[...] = jnp.where(lane >= ROT_DIM // 2, sn, 0.0)


def _rope_tables(positions):
    n = positions.size
    half = ROT_DIM // 2
    inv = ROPE_THETA ** (-np.arange(0, ROT_DIM, 2, dtype=np.float32) / ROT_DIM)
    inv_lane = np.zeros((1, LANES), np.float32)
    inv_lane[0, :half] = inv
    inv_lane[0, half:ROT_DIM] = inv
    tm = _pick(n, (2048, 1024, 512, 256, 128))
    out = jax.ShapeDtypeStruct((n, LANES), F32)
    return pl.pallas_call(
        _rope_kernel,
        grid=(n // tm,),
        in_specs=[pl.BlockSpec((tm, 1), lambda i: (i, 0)),
                  pl.BlockSpec((1, LANES), lambda i: (0, 0))],
        out_specs=[pl.BlockSpec((tm, LANES), lambda i: (i, 0))] * 3,
        out_shape=[out, out, out],
        compiler_params=_params("parallel"),
        name="rope_tables",
    )(positions.reshape(n, 1), jnp.asarray(inv_lane))


def _norm_kernel(x_ref, g_ref, o_ref):
    o_ref[...] = _rms(x_ref[...], g_ref[...]).astype(o_ref.dtype)


def _norm(x, g, dtype):
    n, d = x.shape
    tm = _pick(n, (512, 256, 128))
    return pl.pallas_call(
        _norm_kernel,
        grid=(n // tm,),
        in_specs=[pl.BlockSpec((tm, d), lambda i: (i, 0)),
                  pl.BlockSpec((1, d), lambda i: (0, 0))],
        out_specs=pl.BlockSpec((tm, d), lambda i: (i, 0)),
        out_shape=jax.ShapeDtypeStruct((n, d), dtype),
        compiler_params=_params("parallel"),
        name="row_norm",
    )(x, g.reshape(1, d))


def _proj_kernel(a_ref, w_ref, *rest, q_scale, rotary):
    if rotary:
        c_ref, s1_ref, s2_ref, o_ref, wb_ref = rest
    else:
        o_ref, wb_ref = rest
    j = pl.program_id(0)

    @pl.when(pl.program_id(1) == 0)
    def _():
        wb_ref[...] = w_ref[...].astype(BF16)

    def emit(rope, scale):
        if rope:
            c = c_ref[...]
            s1 = s1_ref[...]
            s2 = s2_ref[...]
        for cols in _chunks(wb_ref.shape[1]):
            acc = jnp.dot(a_ref[...], wb_ref[:, cols], preferred_element_type=F32)
            for h in range((cols.stop - cols.start) // LANES):
                t = acc[:, h * LANES:(h + 1) * LANES]
                if rope:
                    lo = pltpu.roll(t, LANES - ROT_DIM // 2, 1)
                    hi = pltpu.roll(t, ROT_DIM // 2, 1)
                    t = (t * c + lo * s1 + hi * s2) * scale
                o_ref[cols.start // LANES + h] = t.astype(o_ref.dtype)

    if rotary:
        pl.when(j == 0)(lambda: emit(True, q_scale))
        pl.when(j == 1)(lambda: emit(True, 1.0))
        pl.when(j >= 2)(lambda: emit(False, 1.0))
    else:
        emit(False, 1.0)


def _project(h, w_in, layer, tabs, width_b):
    n, d = h.shape
    p = w_in.shape[2]
    tm = _pick(n, (1024, 512, 256, 128))
    q_scale = math.log2(math.e) * HEAD_DIM ** -0.5

    def call(tn, col0, n_tiles, rotary, dtype, name):
        assert col0 % tn == 0
        hpt = tn // LANES
        in_specs = [pl.BlockSpec((tm, d), lambda j, i: (i, 0)),
                    pl.BlockSpec((None, d, tn), lambda j, i: (layer, 0, col0 // tn + j))]
        args = [h, w_in]
        if rotary:
            in_specs += [pl.BlockSpec((tm, LANES), lambda j, i: (i, 0))] * 3
            args += list(tabs)
        return pl.pallas_call(
            functools.partial(_proj_kernel, q_scale=q_scale, rotary=rotary),
            grid=(n_tiles, n // tm),
            in_specs=in_specs,
            out_specs=pl.BlockSpec((hpt, tm, LANES), lambda j, i: (j, i, 0)),
            out_shape=jax.ShapeDtypeStruct((n_tiles * hpt, n, LANES), dtype),
            scratch_shapes=[pltpu.VMEM((d, tn), BF16)],
            compiler_params=_params("arbitrary", "arbitrary"),
            name=name,
        )(*args)

    return call(width_b, 0, 3, True, BF16, "in_proj_qkv")


def _attn_kernel(q_ref, k_ref, v_ref, g_ref, o_ref, qd_ref, kd_ref, vd_ref, bias_ref, acc_ref, m_ref, l_ref, *,
                 seq, tq, unroll):
    win = tq + 2 * RADIUS
    copy_rows = 256
    qq = lax.broadcasted_iota(jnp.int32, (tq, win), 0)
    kk = lax.broadcasted_iota(jnp.int32, (tq, win), 1)
    for case in range(3):
        bias_ref[case] = jnp.where(jnp.abs(kk - qq - case * RADIUS) <= RADIUS, 0.0, NEG)
    ones = jnp.ones((win, LANES), BF16)

    for d in DILATIONS:
        sub_len = seq // d
        nt = sub_len // tq
        first = d == DILATIONS[0]
        last = d == DILATIONS[-1]

        per = copy_rows // d
        ii = lax.broadcasted_iota(jnp.int32, (copy_rows, copy_rows), 0)
        jj = lax.broadcasted_iota(jnp.int32, (copy_rows, copy_rows), 1)
        perm = (jj == (ii % per) * d + ii // per).astype(BF16)

        def deinterleave(bi, carry, d=d, sub_len=sub_len, per=per, perm=perm):
            rows = pl.ds(pl.multiple_of(bi * copy_rows, copy_rows), copy_rows)
            x = jnp.concatenate([ref[rows, :] for ref in (q_ref, k_ref, v_ref)], axis=1)
            y = jnp.dot(perm, x, preferred_element_type=F32).astype(BF16)
            for a, dst_ref in enumerate((qd_ref, kd_ref, vd_ref)):
                for r in range(d):
                    dst = pl.ds(pl.multiple_of(r * sub_len + bi * per, per), per)
                    dst_ref[dst, :] = y[r * per:(r + 1) * per, a * LANES:(a + 1) * LANES]
            return carry

        if d > 1:
            lax.fori_loop(0, seq // copy_rows, deinterleave, 0, unroll=4)
        qs_ref, ks_ref, vs_ref = (q_ref, k_ref, v_ref) if d == 1 else (qd_ref, kd_ref, vd_ref)

        def tile(t, d=d, sub_len=sub_len, nt=nt, first=first, qs_ref=qs_ref, ks_ref=ks_ref, vs_ref=vs_ref):
            r = t // nt
            ti = t % nt
            q0 = ti * tq
            k0 = jnp.clip(q0 - RADIUS, 0, sub_len - win)
            case = jnp.where(ti == 0, 0, jnp.where(ti == nt - 1, 2, 1))
            kbase = pl.multiple_of(r * sub_len + k0, RADIUS)
            if d == 1:
                rows = pl.ds(pl.multiple_of(q0, tq), tq)
            else:
                rows = pl.ds(r + d * q0, tq, stride=d)
            q = qs_ref[pl.ds(pl.multiple_of(t * tq, tq), tq), :]
            k = ks_ref[pl.ds(kbase, win), :]
            v = vs_ref[pl.ds(kbase, win), :]
            s = lax.dot_general(q, k, (((1,), (1,)), ((), ())), preferred_element_type=F32) + bias_ref[case]
            m_t = jnp.broadcast_to(jnp.max(s, axis=-1, keepdims=True), (tq, LANES))
            m_prev = None if first else m_ref[rows, :]
            m_new = m_t if first else jnp.maximum(m_prev, m_t)
            p = jnp.exp2(s - jnp.concatenate([m_new] * (win // LANES), axis=1)).astype(BF16)
            pv = jnp.dot(p, jnp.concatenate([v, ones], axis=1), preferred_element_type=F32)
            return rows, m_prev, m_new, pv[:, LANES:], pv[:, :LANES]

        def merge(rows, m_prev, m_new, l_t, acc_t, first=first, last=last):
            if first:
                l_new, acc_new = l_t, acc_t
            else:
                a = jnp.exp2(m_prev - m_new)
                l_new = a * l_ref[rows, :] + l_t
                acc_new = a * acc_ref[rows, :] + acc_t
            if last:
                o_ref[rows, :] = _rms(acc_new / l_new, g_ref[...]).astype(o_ref.dtype)
            else:
                m_ref[rows, :] = m_new
                l_ref[rows, :] = l_new
                acc_ref[rows, :] = acc_new

        def body(it, carry):
            parts = [tile(it * unroll + u) for u in range(unroll)]
            for part in parts:
                merge(*part)
            return carry

        lax.fori_loop(0, seq // (tq * unroll), body, 0)


def _attention(proj, g_out, batch, seq, n_heads_a, n_heads_b):
    tq = 128
    unroll = 16
    win = tq + 2 * RADIUS
    assert seq % (max(DILATIONS) * tq) == 0 and (seq // tq) % unroll == 0 and seq // max(DILATIONS) >= max(win, 256)
    n = batch * seq
    kern = functools.partial(_attn_kernel, seq=seq, tq=tq, unroll=unroll)
    blk = lambda off: pl.BlockSpec((None, seq, LANES), lambda b, h: (off + h, b, 0))
    return pl.pallas_call(
        kern,
        grid=(batch, n_heads_b),
        in_specs=[blk(0), blk(n_heads_b), blk(2 * n_heads_b),
                  pl.BlockSpec((None, 1, LANES), lambda b, h: (n_heads_a + h, 0, 0))],
        out_specs=pl.BlockSpec((seq, LANES), lambda b, h: (b, h)),
        out_shape=jax.ShapeDtypeStruct((n, n_heads_b * LANES), BF16),
        scratch_shapes=[pltpu.VMEM((seq, LANES), BF16)] * 3
        + [pltpu.VMEM((3, tq, win), F32)]
        + [pltpu.VMEM((seq, LANES), F32)] * 3,
        compiler_params=_params("parallel", "parallel"),
        name="dilated_attention",
    )(proj, proj, proj, g_out.reshape(-1, 1, LANES))


def _gmlp_kernel(h_ref, wu_ref, wv_ref, ws_ref, bs_ref, gv_ref, go_ref, o_ref, wub_ref, wvb_ref, *, n_heads_a,
                 n_chunks):
    @pl.when(pl.program_id(0) == 0)
    def _():
        wub_ref[...] = wu_ref[...].astype(BF16)
        wvb_ref[...] = wv_ref[...].astype(BF16)

    u_all = jnp.dot(h_ref[...], wub_ref[...], preferred_element_type=F32)
    v_all = jnp.dot(h_ref[...], wvb_ref[...], preferred_element_type=F32)
    for h in range(n_heads_a):
        w = ws_ref[h].astype(BF16)
        b = bs_ref[h]
        gv = gv_ref[h]
        go = go_ref[h]
        cols = slice(h * LANES, (h + 1) * LANES)
        for c in range(n_chunks):
            rows = slice(c * CHUNK, (c + 1) * CHUNK)
            u = _gelu(u_all[rows, cols])
            v = _rms(_gelu(v_all[rows, cols]), gv)
            sv = jnp.dot(w, v.astype(BF16), preferred_element_type=F32) + b
            o_ref[rows, cols] = _rms(u * sv, go).astype(o_ref.dtype)


def _gmlp(h, w_in, layer, col0, w_s, b_s, g_v, g_out, n_heads_a):
    n, d = h.shape
    width_a = n_heads_a * LANES
    assert col0 % width_a == 0
    tt = _pick(n, (512, 256, 128))
    kern = functools.partial(_gmlp_kernel, n_heads_a=n_heads_a, n_chunks=tt // CHUNK)
    full = lambda shp: pl.BlockSpec(shp, lambda i: (0,) * len(shp))
    w_spec = lambda blk: pl.BlockSpec((None, d, width_a), lambda i: (layer, 0, col0 // width_a + blk),
                                      pipeline_mode=pl.Buffered(1))
    return pl.pallas_call(
        kern,
        grid=(n // tt,),
        in_specs=[pl.BlockSpec((tt, d), lambda i: (i, 0)),
                  w_spec(0), w_spec(1),
                  full((n_heads_a, CHUNK, CHUNK)),
                  full((n_heads_a, CHUNK, 1)),
                  full((n_heads_a, 1, LANES)),
                  full((n_heads_a, 1, LANES))],
        out_specs=pl.BlockSpec((tt, n_heads_a * LANES), lambda i: (i, 0)),
        out_shape=jax.ShapeDtypeStruct((n, n_heads_a * LANES), BF16),
        scratch_shapes=[pltpu.VMEM((d, width_a), BF16)] * 2,
        compiler_params=_params("arbitrary"),
        name="gmlp_gating",
    )(h, w_in, w_in, w_s, b_s.reshape(n_heads_a, CHUNK, 1), g_v.reshape(n_heads_a, 1, LANES),
      g_out[:n_heads_a].reshape(n_heads_a, 1, LANES))


def _route(hn, r_ref, n_experts):
    r = r_ref[...]
    h_hi = hn.astype(BF16)
    h_lo = (hn - h_hi.astype(F32)).astype(BF16)
    r_hi = r.astype(BF16)
    r_lo = (r - r_hi.astype(F32)).astype(BF16)
    logits = (jnp.dot(h_hi, r_hi, preferred_element_type=F32) + jnp.dot(h_lo, r_hi, preferred_element_type=F32)
              + jnp.dot(h_hi, r_lo, preferred_element_type=F32))
    lane = lax.broadcasted_iota(jnp.int32, logits.shape, 1)
    lg = jnp.where(lane < n_experts, logits, -jnp.inf)
    v1 = jnp.max(lg, axis=-1, keepdims=True)
    i1 = jnp.min(jnp.where(lg == v1, lane, LANES), axis=-1, keepdims=True)
    lg2 = jnp.where(lane == i1, -jnp.inf, lg)
    v2 = jnp.max(lg2, axis=-1, keepdims=True)
    i2 = jnp.min(jnp.where(lg2 == v2, lane, LANES), axis=-1, keepdims=True)
    e2 = jnp.exp(v2 - v1)
    g1 = 1.0 / (1.0 + e2)
    g2 = e2 / (1.0 + e2)
    return jnp.where(lane == 0, i1.astype(F32),
                     jnp.where(lane == 1, i2.astype(F32),
                               jnp.where(lane == 2, g1, jnp.where(lane == 3, g2, 0.0))))


def _store_packed(hn, h_ref):
    tm, d = hn.shape
    wpt = d // (2 * LANES)
    bits = pltpu.bitcast(hn.astype(BF16).astype(F32), U32)
    for c in range(wpt):
        lo = bits[:, c * LANES:(c + 1) * LANES] >> 16
        hi = bits[:, (c + wpt) * LANES:(c + wpt + 1) * LANES] & jnp.uint32(0xFFFF0000)
        h_ref[pl.ds(c, tm, stride=wpt), :] = lo | hi


def _finish(xn, g_ref, r_ref, xo_ref, h_ref, route_ref, n_experts):
    if xo_ref is not None:
        xo_ref[...] = xn
    hn = _rms(xn, g_ref[...])
    if route_ref is None:
        h_ref[...] = hn.astype(h_ref.dtype)
    else:
        _store_packed(hn, h_ref)
        route_ref[...] = _route(hn, r_ref, n_experts)


def _out_kernel(*refs, n_experts):
    if n_experts:
        ya_ref, yb_ref, w_ref, x_ref, g_ref, r_ref, xo_ref, h_ref, route_ref, wb_ref = refs
    else:
        ya_ref, yb_ref, w_ref, x_ref, g_ref, xo_ref, h_ref, wb_ref = refs
        r_ref = route_ref = None

    @pl.when(pl.program_id(0) == 0)
    def _():
        wb_ref[...] = w_ref[...].astype(BF16)

    wa = ya_ref.shape[1]
    acc = jnp.dot(ya_ref[...], wb_ref[:wa, :], preferred_element_type=F32)
    acc += jnp.dot(yb_ref[...], wb_ref[wa:, :], preferred_element_type=F32)
    _finish(x_ref[...] + acc, g_ref, r_ref, xo_ref, h_ref, route_ref, n_experts)


def _pad_router(router):
    d, e = router.shape
    return jnp.zeros((d, LANES), F32).at[:, :e].set(router)


def _out_proj(ya, yb, w_out, layer, x, g, router):
    n, d = x.shape
    tm = _pick(n, (256, 128))
    row = lambda w: pl.BlockSpec((tm, w), lambda i: (i, 0))
    in_specs = [row(ya.shape[1]), row(yb.shape[1]),
                pl.BlockSpec((None, d, d), lambda i: (layer, 0, 0), pipeline_mode=pl.Buffered(1)),
                row(d), pl.BlockSpec((1, d), lambda i: (0, 0))]
    args = [ya, yb, w_out, x, g.reshape(1, d)]
    n_experts = 0
    if router is None:
        out_specs = [row(d), row(d)]
        out_shape = [jax.ShapeDtypeStruct((n, d), F32), jax.ShapeDtypeStruct((n, d), BF16)]
    else:
        n_experts = router.shape[1]
        wpt = d // (2 * LANES)
        in_specs.append(pl.BlockSpec((d, LANES), lambda i: (0, 0)))
        args.append(_pad_router(router))
        out_specs = [row(d), pl.BlockSpec((tm * wpt, LANES), lambda i: (i, 0)), row(LANES)]
        out_shape = [jax.ShapeDtypeStruct((n, d), F32), jax.ShapeDtypeStruct((n * wpt, LANES), U32),
                     jax.ShapeDtypeStruct((n, LANES), F32)]
    return pl.pallas_call(
        functools.partial(_out_kernel, n_experts=n_experts),
        grid=(n // tm,),
        in_specs=in_specs,
        out_specs=out_specs,
        out_shape=out_shape,
        scratch_shapes=[pltpu.VMEM((d, d), BF16)],
        compiler_params=_params("arbitrary"),
        name="out_proj",
    )(*args)


def _up_kernel(te_ref, first_ref, act_ref, look_ref, a_ref, w1_ref, w3_ref, *rest, side_cast):
    if side_cast:
        w2_ref, o_ref, w2b_ref, w1b_ref, w3b_ref = rest
        w2b_ref[...] = w2_ref[...].astype(BF16)
    else:
        o_ref, w1b_ref, w3b_ref = rest
    i = pl.program_id(1)

    @pl.when(first_ref[i] == 1)
    def _():
        w1b_ref[...] = w1_ref[...].astype(BF16)
        w3b_ref[...] = w3_ref[...].astype(BF16)

    @pl.when(act_ref[i] == 1)
    def _():
        for cols in _chunks(o_ref.shape[1]):
            h1 = jnp.dot(a_ref[...], w1b_ref[:, cols], preferred_element_type=F32)
            h3 = jnp.dot(a_ref[...], w3b_ref[:, cols], preferred_element_type=F32)
            o_ref[:, cols] = (_silu(h1) * h3).astype(o_ref.dtype)

    @pl.when(act_ref[i] == 0)
    def _():
        o_ref[...] = jnp.zeros_like(o_ref)


def _up(a, w1, w3, layer, plan, tm, w2=None):
    r, d = a.shape
    f = w1.shape[3]
    tn = _pick(f, (1024, 512, 256, 128))
    n_tiles = r // tm
    te, first, act = plan
    nxt = jnp.concatenate([te[1:], te[-1:]])
    look = jnp.where(first == 0, nxt, te)
    plan = (te, first, act, look)
    w1_spec = pl.BlockSpec((None, None, d, tn), lambda j, i, te, first, act, look: (layer, look[i], 0, j))
    w3_spec = pl.BlockSpec((None, None, d, tn), lambda j, i, te, first, act, look: (layer, te[i], 0, j))
    in_specs = [pl.BlockSpec((tm, d), lambda j, i, *_: (i, 0)), w1_spec, w3_spec]
    out_specs = [pl.BlockSpec((tm, tn), lambda j, i, *_: (i, j))]
    out_shape = [jax.ShapeDtypeStruct((r, f), BF16)]
    args = [a, w1, w3]
    if w2 is not None:
        n_layers, n_experts, _, d_out = w2.shape
        rows = n_experts * f
        steps = (f // tn) * n_tiles
        slab = next(sl for sl in (128, 256, 512, 1024, 2048) if rows % sl == 0 and rows // sl <= steps)
        n_slabs = rows // slab
        slab_idx = lambda j, i, *_: jnp.minimum(j * n_tiles + i, n_slabs - 1)
        in_specs.append(pl.BlockSpec((None, slab, d_out), lambda *g: (layer, slab_idx(*g), 0)))
        out_specs.append(pl.BlockSpec((slab, d_out), lambda *g: (slab_idx(*g), 0)))
        out_shape.append(jax.ShapeDtypeStruct((rows, d_out), BF16))
        args.append(w2.reshape(n_layers, rows, d_out))
    grid_spec = pltpu.PrefetchScalarGridSpec(
        num_scalar_prefetch=4,
        grid=(f // tn, n_tiles),
        in_specs=in_specs,
        out_specs=out_specs,
        scratch_shapes=[pltpu.VMEM((d, tn), BF16)] * 2,
    )
    outs = pl.pallas_call(
        functools.partial(_up_kernel, side_cast=w2 is not None),
        grid_spec=grid_spec,
        out_shape=out_shape,
        compiler_params=_params("arbitrary", "arbitrary"),
        name="swiglu_up",
    )(*plan, *args)
    if w2 is None:
        return outs[0]
    return outs[0], outs[1].reshape(n_experts, f, d_out)


def _down_kernel(te_ref, first_ref, act_ref, a_ref, w_ref, *rest, residual, cast, norm):
    rest = list(rest)
    x_ref = rest.pop(0) if residual else None
    g_ref = rest.pop(0) if norm else None
    o_ref = rest.pop(0)
    h_ref = rest.pop(0) if norm else None
    i = pl.program_id(1)
    if cast:
        wb_ref = rest.pop(0)

        @pl.when(first_ref[i] == 1)
        def _():
            wb_ref[...] = w_ref[...].astype(BF16)
    else:
        wb_ref = w_ref

    @pl.when(act_ref[i] == 1)
    def _():
        for cols in _chunks(o_ref.shape[1], DOWN_COL_CHUNK):
            acc = jnp.dot(a_ref[...], wb_ref[:, cols], preferred_element_type=F32)
            o_ref[:, cols] = x_ref[:, cols] + acc if residual else acc
        if norm:
            h_ref[...] = _rms(o_ref[...], g_ref[...]).astype(h_ref.dtype)

    @pl.when(act_ref[i] == 0)
    def _():
        o_ref[...] = jnp.zeros_like(o_ref)
        if norm:
            h_ref[...] = jnp.zeros_like(h_ref)


def _down(a, w2, layer, plan, tm, x=None, norm=None):
    r, f = a.shape
    d = w2.shape[3]
    cast = w2.dtype != BF16

    def vmem_bytes(tn, w_bufs):
        out_blocks = 4 * (2 if x is not None else 1) + (4 if norm else 0)
        io_blocks = 2 * (tm * f * 2 + tm * tn * out_blocks)
        w_bytes = f * tn * (4 * w_bufs + 2) if cast else f * tn * 2 * w_bufs
        return w_bytes + io_blocks

    single_group = w2.shape[1] == 1
    order = [(d, 1)] if single_group else []
    order += [(t, b) for t in (1024, 512, 256, 128) for b in (2, 1)]
    tn, w_bufs = next((t, b) for t, b in order if d % t == 0 and vmem_bytes(t, b) <= DOWN_VMEM_BUDGET_BYTES)
    fuse_norm = norm is not None and tn == d
    w_mode = {} if w_bufs == 2 else {"pipeline_mode": pl.Buffered(1)}
    blk = pl.BlockSpec((tm, tn), lambda j, i, te, first, act: (i, j))
    in_specs = [pl.BlockSpec((tm, f), lambda j, i, te, first, act: (i, 0)),
                pl.BlockSpec((None, None, f, tn), lambda j, i, te, first, act: (layer, te[i], 0, j), **w_mode)]
    args = [a, w2]
    out_specs = [blk]
    out_shape = [jax.ShapeDtypeStruct((r, d), F32)]
    if x is not None:
        in_specs.append(blk)
        args.append(x)
    if fuse_norm:
        in_specs.append(pl.BlockSpec((1, d), lambda j, i, te, first, act: (0, 0)))
        args.append(norm[0].reshape(1, d))
        out_specs.append(blk)
        out_shape.append(jax.ShapeDtypeStruct((r, d), norm[1]))
    grid_spec = pltpu.PrefetchScalarGridSpec(
        num_scalar_prefetch=3,
        grid=(d // tn, r // tm),
        in_specs=in_specs,
        out_specs=out_specs,
        scratch_shapes=[pltpu.VMEM((f, tn), BF16)] if cast else [],
    )
    outs = pl.pallas_call(
        functools.partial(_down_kernel, residual=x is not None, cast=cast, norm=fuse_norm),
        grid_spec=grid_spec,
        out_shape=out_shape,
        compiler_params=_params("arbitrary", "arbitrary"),
        name="swiglu_down",
    )(*plan, *args)
    return outs[0], (outs[1] if fuse_norm else None)


def _single_group_plan(n_tiles):
    te = jnp.zeros((n_tiles,), jnp.int32)
    first = jnp.zeros((n_tiles,), jnp.int32).at[0].set(1)
    act = jnp.ones((n_tiles,), jnp.int32)
    return te, first, act


def _gather_kernel(src_ref, h_hbm, o_ref, buf_ref, sem, *, tg, wpt):
    i = pl.program_id(0)
    n_blocks = pl.num_programs(0)

    def row_copy(block, slot, r):
        tok = src_ref[block * tg + r]
        return pltpu.make_async_copy(h_hbm.at[pl.ds(tok * wpt, wpt)],
                                     buf_ref.at[pl.ds((slot * tg + r) * wpt, wpt)], sem.at[slot])

    def start_block(block, slot):
        def start(r, c):
            row_copy(block, slot, r).start()
            return c
        lax.fori_loop(0, tg, start, 0, unroll=8)

    @pl.when(i == 0)
    def _():
        start_block(0, 0)

    @pl.when(i + 1 < n_blocks)
    def _():
        start_block(i + 1, (i + 1) % 2)

    slot = i % 2

    def wait(r, c):
        row_copy(i, slot, r).wait()
        return c
    lax.fori_loop(0, tg, wait, 0, unroll=8)

    for c in range(wpt):
        w = buf_ref[pl.ds(slot * tg * wpt + c, tg, stride=wpt), :]
        o_ref[:, c * LANES:(c + 1) * LANES] = pltpu.bitcast(w << 16, F32).astype(o_ref.dtype)
        o_ref[:, (c + wpt) * LANES:(c + wpt + 1) * LANES] = (
            pltpu.bitcast(w & jnp.uint32(0xFFFF0000), F32).astype(o_ref.dtype))


def _gather_rows(hp, src, d):
    rp = src.shape[0]
    wpt = d // (2 * LANES)
    tg = _pick(rp, (256, 128))
    grid_spec = pltpu.PrefetchScalarGridSpec(
        num_scalar_prefetch=1,
        grid=(rp // tg,),
        in_specs=[pl.BlockSpec(memory_space=pl.ANY)],
        out_specs=pl.BlockSpec((tg, d), lambda i, src: (i, 0)),
        scratch_shapes=[pltpu.VMEM((2 * tg * wpt, LANES), U32), pltpu.SemaphoreType.DMA((2,))],
    )
    return pl.pallas_call(
        functools.partial(_gather_kernel, tg=tg, wpt=wpt),
        grid_spec=grid_spec,
        out_shape=jax.ShapeDtypeStruct((rp, d), BF16),
        compiler_params=_params("arbitrary"),
        name="moe_gather",
    )(src, hp)


def _combine_kernel(pos_ref, x_ref, route_ref, ys_hbm, g_ref, *rest, tt, keep_x):
    if keep_x:
        xo_ref, h_ref, buf_ref, sem = rest
    else:
        xo_ref = None
        h_ref, buf_ref, sem = rest
    i = pl.program_id(0)
    n_blocks = pl.num_programs(0)

    def row_copy(block, slot, r, k):
        return pltpu.make_async_copy(ys_hbm.at[pl.ds(pos_ref[TOP_K * (block * tt + r) + k], 1)],
                                     buf_ref.at[slot, k, pl.ds(r, 1)], sem.at[slot, k])

    def start_block(block, slot):
        def start(r, c):
            for k in range(TOP_K):
                row_copy(block, slot, r, k).start()
            return c
        lax.fori_loop(0, tt, start, 0, unroll=4)

    @pl.when(i == 0)
    def _():
        start_block(0, 0)

    @pl.when(i + 1 < n_blocks)
    def _():
        start_block(i + 1, (i + 1) % 2)

    slot = i % 2

    def wait(r, c):
        for k in range(TOP_K):
            row_copy(i, slot, r, k).wait()
        return c

    lax.fori_loop(0, tt, wait, 0, unroll=4)
    route = route_ref[...]
    xn = x_ref[...] + route[:, 2:3] * buf_ref[slot, 0] + route[:, 3:4] * buf_ref[slot, 1]
    _finish(xn, g_ref, None, xo_ref, h_ref, None, 0)


def _combine(x, route, ys, pos, g, h_dtype, keep_x):
    n, d = x.shape
    tt = _pick(n, (256, 128))
    row = pl.BlockSpec((tt, d), lambda i, pos: (i, 0))
    out_specs = [row, row] if keep_x else [row]
    out_shape = [jax.ShapeDtypeStruct((n, d), h_dtype)]
    if keep_x:
        out_shape.insert(0, jax.ShapeDtypeStruct((n, d), F32))
    grid_spec = pltpu.PrefetchScalarGridSpec(
        num_scalar_prefetch=1,
        grid=(n // tt,),
        in_specs=[row, pl.BlockSpec((tt, LANES), lambda i, pos: (i, 0)),
                  pl.BlockSpec(memory_space=pl.ANY),
                  pl.BlockSpec((1, d), lambda i, pos: (0, 0))],
        out_specs=out_specs,
        scratch_shapes=[pltpu.VMEM((2, TOP_K, tt, d), F32), pltpu.SemaphoreType.DMA((2, TOP_K))],
    )
    outs = pl.pallas_call(
        functools.partial(_combine_kernel, tt=tt, keep_x=keep_x),
        grid_spec=grid_spec,
        out_shape=out_shape,
        compiler_params=_params("arbitrary"),
        name="moe_combine",
    )(pos, x, route, ys, g.reshape(1, d))
    return (outs[0], outs[1]) if keep_x else (None, outs[0])


def _tile_plan(gend, n_experts, rp, tm):
    start = jnp.arange(rp // tm, dtype=jnp.int32) * tm
    te = jnp.minimum(jnp.sum((start[:, None] >= gend[None, :]).astype(jnp.int32), axis=1), n_experts - 1)
    prev = jnp.concatenate([jnp.full((1,), -1, jnp.int32), te[:-1]])
    first = (te != prev).astype(jnp.int32)
    act = (start < gend[-1]).astype(jnp.int32)
    return te, first, act


def _dispatch_plan(route, n_experts, pad):
    n = route.shape[0]
    ef = route[:, :TOP_K].astype(jnp.int32).reshape(-1)
    onehot = (ef[:, None] == jnp.arange(n_experts, dtype=jnp.int32)[None, :]).astype(jnp.int32)
    csum = jnp.cumsum(onehot, axis=0)
    counts = csum[-1]
    rank = jnp.take_along_axis(csum, ef[:, None], axis=1)[:, 0] - 1
    padded = ((counts + pad - 1) // pad) * pad
    gend = jnp.cumsum(padded)
    gstart = gend - padded
    pos = (gstart[ef] + rank).astype(jnp.int32)
    rp = TOP_K * n + n_experts * pad
    src = jnp.zeros((rp,), jnp.int32).at[pos].set(jnp.arange(TOP_K * n, dtype=jnp.int32) // TOP_K)
    return pos, src, gend, rp


def kernel(x, positions, attn_norm, w_in, w_s, b_s, gmlp_norm, out_norm, w_out, ffn_norm, dense_w1, dense_w3,
           dense_w2, router, moe_w1, moe_w3, moe_w2, final_norm):
    batch, seq, d = x.shape
    depth = w_in.shape[0]
    n = batch * seq
    n_heads = d // HEAD_DIM
    n_heads_a = n_heads // 4
    n_heads_b = n_heads - n_heads_a
    width_b = n_heads_b * HEAD_DIM
    tm_up = _pick(n, (512, 256, 128))
    tm_down = _pick(n, (256, 128))

    xf = x.reshape(n, d)
    tabs = _rope_tables(positions)
    h = _norm(xf, attn_norm[0], BF16)
    for i in range(depth):
        moe = i % 2 == 1
        j = i // 2
        last = i == depth - 1
        next_g = final_norm if last else attn_norm[i + 1]
        next_dtype = F32 if last else BF16

        qkv = _project(h, w_in, i, tabs, width_b)
        yb = _attention(qkv, out_norm[i], batch, seq, n_heads_a, n_heads_b)
        ya = _gmlp(h, w_in, i, 3 * width_b, w_s[i], b_s[i], gmlp_norm[i], out_norm[i], n_heads_a)
        if not moe:
            xf, h2 = _out_proj(ya, yb, w_out, i, xf, ffn_norm[i], None)
            tm_dense = _pick(n, (1024, 512, 256, 128))
            mid, w2b = _up(h2, dense_w1[:, None], dense_w3[:, None], j, _single_group_plan(n // tm_dense), tm_dense,
                           w2=dense_w2[:, None])
            xf, h = _down(mid, w2b[None], 0, _single_group_plan(n // tm_down), tm_down, x=xf,
                          norm=(next_g, next_dtype))
            if h is None:
                h = _norm(xf, next_g, next_dtype)
        else:
            n_experts = router.shape[2]
            xf, h2p, route = _out_proj(ya, yb, w_out, i, xf, ffn_norm[i], router[j])
            pos, src, gend, rp = _dispatch_plan(route, n_experts, tm_up)
            xs = _gather_rows(h2p, src, d)
            mid, w2b = _up(xs, moe_w1, moe_w3, j, _tile_plan(gend, n_experts, rp, tm_up), tm_up, w2=moe_w2)
            ys, _ = _down(mid, w2b[None], 0, _tile_plan(gend, n_experts, rp, tm_down), tm_down)
            xf, h = _combine(xf, route, ys, pos, next_g, next_dtype, keep_x=not last)
    return h.reshape(batch, seq, d)
```
